```python
import jax, jax.numpy as jnp
from jax import lax
import numpy as np

D_MODEL = 1024
BATCH = 1
SEQ = 16384
DEPTH = 1
DEC_BATCH = 4
DEC_SEQ = 4096
PAST_LEN = 128

HEAD_DIM = 64
ATT_GROUPS = ((128, 1), (512, 4), (2048, 16))
N_GROUPS = 3
HEADS_PER_GROUP = 4
N_ATT_HEADS = N_GROUPS * HEADS_PER_GROUP
ATT_QKV_WIDTH = N_ATT_HEADS * HEAD_DIM
ATT_OUT_WIDTH = HEADS_PER_GROUP * HEAD_DIM
Q_BLOCK = 128
ATT_SCALE = HEAD_DIM ** -0.5
NEG_INF = -1e30
N_BUCKETS = 32
MAX_DISTANCE = 1024
CHUNK = 128
SGU_GROUPS = 4
SGU_GROUP_DIM = 128
SGU_WIDTH = SGU_GROUPS * SGU_GROUP_DIM
D_FF = 2816
EPS = 1e-6
IN_COLS = 3 * ATT_QKV_WIDTH + 2 * SGU_WIDTH + 2 * D_MODEL

kernel_name = "hybrid_dilated_attn_gmlp_encoder"


def _rmsnorm(x, g):
    xf = x.astype(jnp.float32)
    y = xf * lax.rsqrt(jnp.mean(xf * xf, axis=-1, keepdims=True) + EPS)
    return (y * g.astype(jnp.float32)).astype(x.dtype)


def _layernorm(x, g, b):
    xf = x.astype(jnp.float32)
    mu = jnp.mean(xf, axis=-1, keepdims=True)
    var = jnp.mean(jnp.square(xf - mu), axis=-1, keepdims=True)
    y = (xf - mu) * lax.rsqrt(var + EPS)
    return (y * g.astype(jnp.float32) + b.astype(jnp.float32)).astype(x.dtype)


def _swiglu(x, w_gate, w_up, w_down):
    return (jax.nn.silu(x @ w_gate) * (x @ w_up)) @ w_down


def _t5_bucket(rel):
    nb = N_BUCKETS // 2
    max_exact = nb // 2
    ret = (rel > 0).astype(np.int32) * nb
    n = np.abs(rel)
    large = max_exact + (np.log(np.maximum(n, 1) / max_exact) / np.log(MAX_DISTANCE / max_exact)
                         * (nb - max_exact)).astype(np.int32)
    large = np.minimum(large, nb - 1)
    return ret + np.where(n < max_exact, n, large)


def _dilated_attention(q, k, v, rel_bias):
    B, S = q.shape[0], q.shape[1]
    qs = [q[:, :, g] for g in range(N_GROUPS)]
    ks = [k[:, :, g] for g in range(N_GROUPS)]
    vs = [v[:, :, g] for g in range(N_GROUPS)]
    offs, biases = [], []
    for g, (window, dil) in enumerate(ATT_GROUPS):
        half = (window // 2) // dil
        off = np.arange(-half, half + 1, dtype=np.int32) * dil
        offs.append(jnp.asarray(off))
        hs = slice(g * HEADS_PER_GROUP, (g + 1) * HEADS_PER_GROUP)
        biases.append(rel_bias[_t5_bucket(off)][:, hs].T.astype(jnp.float32))
    q_idx = jnp.arange(Q_BLOCK, dtype=jnp.int32)

    def block(start):
        outs, lses = [], []
        for g in range(N_GROUPS):
            qg = lax.dynamic_slice_in_dim(qs[g], start, Q_BLOCK, axis=1)
            pos = start + q_idx[:, None] + offs[g][None, :]
            valid = (pos >= 0) & (pos < S)
            pc = jnp.clip(pos, 0, S - 1)
            kg = ks[g][:, pc]
            vg = vs[g][:, pc]
            logits = (jnp.einsum('bqhd,bqjhd->bhqj', qg, kg).astype(jnp.float32) * ATT_SCALE
                      + biases[g][None, :, None, :])
            logits = jnp.where(valid[None, None], logits, NEG_INF)
            m = jnp.max(logits, axis=-1, keepdims=True)
            e = jnp.exp(logits - m)
            s = jnp.sum(e, axis=-1)
            o = jnp.einsum('bhqj,bqjhd->bqhd', e, vg.astype(jnp.float32)) / jnp.swapaxes(s, 1, 2)[..., None]
            outs.append(o)
            lses.append(m[..., 0] + jnp.log(s))
        alpha = jax.nn.softmax(jnp.stack(lses), axis=0)
        alpha = jnp.swapaxes(alpha, 2, 3)[..., None]
        return jnp.sum(alpha * jnp.stack(outs), axis=0).astype(q.dtype)

    starts = jnp.arange(S // Q_BLOCK, dtype=jnp.int32) * Q_BLOCK
    out = lax.map(block, starts)
    return jnp.moveaxis(out, 0, 1).reshape(B, S, ATT_OUT_WIDTH)


def _spatial_gating(z, ln_g, ln_b, w_s, b_s):
    B, S = z.shape[0], z.shape[1]
    u, vv = jnp.split(z, 2, axis=-1)
    vv = _layernorm(vv, ln_g, ln_b)
    vc = vv.reshape(B, S // CHUNK, CHUNK, SGU_GROUPS, SGU_GROUP_DIM)
    mixed = jnp.einsum('gpq,bcqgd->bcpgd', w_s, vc) + b_s.T[None, None, :, :, None]
    return u * mixed.reshape(B, S, SGU_WIDTH)


def _layer(x, rel_bias, ffn1_pre_g, ffn1_post_g, ffn1_w_gate, ffn1_w_up, ffn1_w_down,
           mix_pre_g, w_in, sgu_ln_g, sgu_ln_b, sgu_w_s, sgu_b_s, w_att, w_sgu, w_out, mix_post_g,
           ffn2_pre_g, ffn2_post_g, ffn2_w_gate, ffn2_w_up, ffn2_w_down, final_g):
    B, S = x.shape[0], x.shape[1]
    h = _swiglu(_rmsnorm(x, ffn1_pre_g), ffn1_w_gate, ffn1_w_up, ffn1_w_down)
    x = x + 0.5 * _rmsnorm(h, ffn1_post_g)
    h = _rmsnorm(x, mix_pre_g)
    proj = h @ w_in
    q, k, v, z, g_a, g_b = jnp.split(proj, np.cumsum([ATT_QKV_WIDTH, ATT_QKV_WIDTH, ATT_QKV_WIDTH,
                                                     2 * SGU_WIDTH, D_MODEL]).tolist(), axis=-1)
    hshape = (B, S, N_GROUPS, HEADS_PER_GROUP, HEAD_DIM)
    att = _dilated_attention(q.reshape(hshape), k.reshape(hshape), v.reshape(hshape), rel_bias)
    sgu = _spatial_gating(jax.nn.gelu(z, approximate=False), sgu_ln_g, sgu_ln_b, sgu_w_s, sgu_b_s)
    merged = jax.nn.sigmoid(g_a) * (att @ w_att) + jax.nn.sigmoid(g_b) * (sgu @ w_sgu)
    x = x + _rmsnorm(merged @ w_out, mix_post_g)
    h = _swiglu(_rmsnorm(x, ffn2_pre_g), ffn2_w_gate, ffn2_w_up, ffn2_w_down)
    x = x + 0.5 * _rmsnorm(h, ffn2_post_g)
    return _rmsnorm(x, final_g)


def _trunk(x, rel_bias, ffn1_pre_g, ffn1_post_g, ffn1_w_gate, ffn1_w_up, ffn1_w_down,
           mix_pre_g, w_in, sgu_ln_g, sgu_ln_b, sgu_w_s, sgu_b_s, w_att, w_sgu, w_out, mix_post_g,
           ffn2_pre_g, ffn2_post_g, ffn2_w_gate, ffn2_w_up, ffn2_w_down, final_g):
    for l in range(DEPTH):
        x = _layer(x, rel_bias, ffn1_pre_g[l], ffn1_post_g[l], ffn1_w_gate[l], ffn1_w_up[l], ffn1_w_down[l],
                   mix_pre_g[l], w_in[l], sgu_ln_g[l], sgu_ln_b[l], sgu_w_s[l], sgu_b_s[l],
                   w_att[l], w_sgu[l], w_out[l], mix_post_g[l],
                   ffn2_pre_g[l], ffn2_post_g[l], ffn2_w_gate[l], ffn2_w_up[l], ffn2_w_down[l], final_g[l])
    return x


def setup_inputs(seed: int = 0) -> dict:
    key = jax.random.key(seed)
    ks = jax.random.split(key, 32)
    f32 = jnp.float32

    def nrm(k, shape, scale):
        return jax.random.normal(k, shape, f32) * scale

    def gain(k, n):
        return 1.0 + 0.02 * jax.random.normal(k, (DEPTH, n), f32)

    L = DEPTH
    return {
        "x_prompt": jax.random.normal(ks[0], (BATCH, SEQ, D_MODEL), f32),
        "x_sample": jax.random.normal(ks[1], (DEC_BATCH, DEC_SEQ, D_MODEL), f32),
        "rel_bias": nrm(ks[2], (N_BUCKETS, N_ATT_HEADS), 0.5),
        "ffn1_pre_g": gain(ks[3], D_MODEL),
        "ffn1_post_g": gain(ks[4], D_MODEL),
        "ffn1_w_gate": nrm(ks[5], (L, D_MODEL, D_FF), D_MODEL ** -0.5),
        "ffn1_w_up": nrm(ks[6], (L, D_MODEL, D_FF), D_MODEL ** -0.5),
        "ffn1_w_down": nrm(ks[7], (L, D_FF, D_MODEL), D_FF ** -0.5),
        "mix_pre_g": gain(ks[8], D_MODEL),
        "w_in": nrm(ks[9], (L, D_MODEL, IN_COLS), D_MODEL ** -0.5),
        "sgu_ln_g": gain(ks[10], SGU_WIDTH),
        "sgu_ln_b": nrm(ks[11], (L, SGU_WIDTH), 0.02),
        "sgu_w_s": nrm(ks[12], (L, SGU_GROUPS, CHUNK, CHUNK), CHUNK ** -0.5),
        "sgu_b_s": 1.0 + nrm(ks[13], (L, SGU_GROUPS, CHUNK), 0.02),
        "w_att": nrm(ks[14], (L, ATT_OUT_WIDTH, D_MODEL), ATT_OUT_WIDTH ** -0.5),
        "w_sgu": nrm(ks[15], (L, SGU_WIDTH, D_MODEL), SGU_WIDTH ** -0.5),
        "w_out": nrm(ks[16], (L, D_MODEL, D_MODEL), D_MODEL ** -0.5),
        "mix_post_g": gain(ks[17], D_MODEL),
        "ffn2_pre_g": gain(ks[18], D_MODEL),
        "ffn2_post_g": gain(ks[19], D_MODEL),
        "ffn2_w_gate": nrm(ks[20], (L, D_MODEL, D_FF), D_MODEL ** -0.5),
        "ffn2_w_up": nrm(ks[21], (L, D_MODEL, D_FF), D_MODEL ** -0.5),
        "ffn2_w_down": nrm(ks[22], (L, D_FF, D_MODEL), D_FF ** -0.5),
        "final_g": gain(ks[23], D_MODEL),
    }


def reference(x_prompt, x_sample, rel_bias, ffn1_pre_g, ffn1_post_g, ffn1_w_gate, ffn1_w_up, ffn1_w_down,
              mix_pre_g, w_in, sgu_ln_g, sgu_ln_b, sgu_w_s, sgu_b_s, w_att, w_sgu, w_out, mix_post_g,
              ffn2_pre_g, ffn2_post_g, ffn2_w_gate, ffn2_w_up, ffn2_w_down, final_g):
    y_prompt = _trunk(x_prompt, rel_bias, ffn1_pre_g, ffn1_post_g, ffn1_w_gate, ffn1_w_up, ffn1_w_down,
                      mix_pre_g, w_in, sgu_ln_g, sgu_ln_b, sgu_w_s, sgu_b_s, w_att, w_sgu, w_out, mix_post_g,
                      ffn2_pre_g, ffn2_post_g, ffn2_w_gate, ffn2_w_up, ffn2_w_down, final_g)
    y_sample = _trunk(x_sample, rel_bias, ffn1_pre_g, ffn1_post_g, ffn1_w_gate, ffn1_w_up, ffn1_w_down,
                      mix_pre_g, w_in, sgu_ln_g, sgu_ln_b, sgu_w_s, sgu_b_s, w_att, w_sgu, w_out, mix_post_g,
                      ffn2_pre_g, ffn2_post_g, ffn2_w_gate, ffn2_w_up, ffn2_w_down, final_g)
    return (y_prompt, y_sample)
```

```python
import functools

import numpy as np
import jax
import jax.numpy as jnp
from jax import lax
from jax.experimental import pallas as pl
from jax.experimental.pallas import tpu as pltpu

_D = 1024
_DFF = 2816
_HEAD_DIM = 64
_HEADS_PER_GROUP = 4
_DILATIONS = (1, 4, 16)
_HALF_KEYS = 64
_GROUP_W = _HEADS_PER_GROUP * _HEAD_DIM
_QKV_W = len(_DILATIONS) * _GROUP_W
_SGU_W = 512
_SGU_GROUPS = 4
_CHUNK = 128
_N_BUCKETS = 32
_MAX_DISTANCE = 1024
_EPS = 1e-6
_NEG_INF = -1e30
_ATT_SCALE = _HEAD_DIM ** -0.5

_LANES = 128
_Q_BLOCK = 128
_K_WINDOW = _Q_BLOCK + 2 * _HALF_KEYS
_FF_CHUNK = 256
_N_FF_CHUNKS = _DFF // _FF_CHUNK
_TOKEN_TILE = 512
_VMEM_LIMIT = 56 * 1024 * 1024

_BF16 = jnp.bfloat16
_F32 = jnp.float32


def _rms(x, g):
    return x * lax.rsqrt(jnp.mean(x * x, axis=-1, keepdims=True) + _EPS) * g


def _dot(a, b):
    return jnp.dot(a, b, preferred_element_type=_F32)


def _const_spec(shape):
    return pl.BlockSpec(shape, lambda *_: (0,) * len(shape), pipeline_mode=pl.Buffered(1))


def _params(n_axes):
    return pltpu.CompilerParams(
        dimension_semantics=("arbitrary",) * n_axes, vmem_limit_bytes=_VMEM_LIMIT)


def _ffn_body(x_ref, pre_ref, post_ref, fin_ref, wg_ref, wu_ref, wd_ref, o_ref, xb_ref, acc_ref,
              *, final):
    x = x_ref[...]
    xb_ref[...] = _rms(x, pre_ref[...]).astype(_BF16)
    for c in range(_N_FF_CHUNKS):
        xb = xb_ref[...]
        gate = _dot(xb, wg_ref[c])
        up = _dot(xb, wu_ref[c])
        act = (gate * jax.nn.sigmoid(gate) * up).astype(_BF16)
        part = _dot(act, wd_ref[c])
        if c == 0:
            acc_ref[...] = part
        else:
            acc_ref[...] += part
    y = x + 0.5 * _rms(acc_ref[...], post_ref[...])
    if final:
        y = _rms(y, fin_ref[...])
    o_ref[...] = y


def _ffn(x, pre_g, post_g, fin_g, wg, wu, wd, *, final):
    n = x.shape[0]
    tm = _TOKEN_TILE
    row = pl.BlockSpec((tm, _D), lambda i: (i, 0))
    return pl.pallas_call(
        functools.partial(_ffn_body, final=final),
        grid=(n // tm,),
        in_specs=[row, _const_spec((1, _D)), _const_spec((1, _D)), _const_spec((1, _D)),
                  _const_spec(wg.shape), _const_spec(wu.shape), _const_spec(wd.shape)],
        out_specs=row,
        out_shape=jax.ShapeDtypeStruct((n, _D), _F32),
        scratch_shapes=[pltpu.VMEM((tm, _D), _BF16), pltpu.VMEM((tm, _D), _F32)],
        compiler_params=_params(1),
        name="ffn_final" if final else "ffn",
    )(x, pre_g, post_g, fin_g, wg, wu, wd)


def _qkv_body(x_ref, g_ref, w_ref, q_ref, k_ref, v_ref):
    hb = _rms(x_ref[...], g_ref[...]).astype(_BF16)
    q_ref[...] = (_dot(hb, w_ref[:, 0:_QKV_W]) * _ATT_SCALE).astype(_BF16)
    k_ref[...] = _dot(hb, w_ref[:, _QKV_W:2 * _QKV_W]).astype(_BF16)
    v_ref[...] = _dot(hb, w_ref[:, 2 * _QKV_W:3 * _QKV_W]).astype(_BF16)


def _qkv(x, pre_g, w):
    n = x.shape[0]
    tm = _TOKEN_TILE
    out = pl.BlockSpec((tm, _QKV_W), lambda i: (i, 0))
    shape = jax.ShapeDtypeStruct((n, _QKV_W), _BF16)
    return pl.pallas_call(
        _qkv_body,
        grid=(n // tm,),
        in_specs=[pl.BlockSpec((tm, _D), lambda i: (i, 0)), _const_spec((1, _D)),
                  _const_spec(w.shape)],
        out_specs=[out, out, out],
        out_shape=[shape, shape, shape],
        compiler_params=_params(1),
        name="qkv",
    )(x, pre_g, w)


def _att_body(q_ref, k_ref, v_ref, bias_ref, o_ref, lse_ref, *, seq):
    i = pl.program_id(2)
    start = jnp.clip(i * _Q_BLOCK - _HALF_KEYS, 0, seq - _K_WINDOW)
    start = pl.multiple_of(start, _HALF_KEYS)
    q = q_ref[...]
    kw = k_ref[pl.ds(start, _K_WINDOW), :]
    vw = v_ref[pl.ds(start, _K_WINDOW), :]
    low_head = lax.broadcasted_iota(jnp.int32, (_Q_BLOCK, _LANES), 1) < _HEAD_DIM
    for pair in range(_GROUP_W // _LANES):
        cols = slice(pair * _LANES, (pair + 1) * _LANES)
        q2, k2, v2 = q[:, cols], kw[:, cols], vw[:, cols]
        outs, lses = [], []
        for sub in range(2):
            keep = low_head if sub == 0 else jnp.logical_not(low_head)
            qh = jnp.where(keep, q2, jnp.zeros_like(q2))
            s = lax.dot_general(qh, k2, (((1,), (1,)), ((), ())), preferred_element_type=_F32)
            s = s + bias_ref[2 * pair + sub]
            m = jnp.max(s, axis=-1, keepdims=True)
            e = jnp.exp(s - m)
            den = jnp.sum(e, axis=-1, keepdims=True)
            outs.append(_dot(e.astype(_BF16), v2) / den)
            lses.append(m + jnp.log(den))
        o_ref[:, cols] = jnp.where(low_head, outs[0], outs[1])
        lse_ref[:, cols] = jnp.where(low_head, lses[0], lses[1])


def _att(q, k, v, bias, col):
    b, d, seq, _ = q.shape
    nb = seq // _Q_BLOCK
    assert seq % _Q_BLOCK == 0 and seq >= _K_WINDOW
    blk = pl.BlockSpec((None, None, _Q_BLOCK, _GROUP_W), lambda bi, r, i: (bi, r, i, col))
    full = pl.BlockSpec((None, None, seq, _GROUP_W), lambda bi, r, i: (bi, r, 0, col))
    out_blk = pl.BlockSpec((None, None, _Q_BLOCK, _GROUP_W), lambda bi, r, i: (bi, r, i, 0))

    def bias_map(bi, r, i):
        return (jnp.where(i == 0, 0, jnp.where(i == nb - 1, 2, 1)), 0, 0, 0)

    shape = jax.ShapeDtypeStruct((b, d, seq, _GROUP_W), _F32)
    return pl.pallas_call(
        functools.partial(_att_body, seq=seq),
        grid=(b, d, nb),
        in_specs=[blk, full, full,
                  pl.BlockSpec((None, _HEADS_PER_GROUP, _Q_BLOCK, _K_WINDOW), bias_map)],
        out_specs=[out_blk, out_blk],
        out_shape=[shape, shape],
        compiler_params=_params(3),
        name=f"att_d{d}",
    )(q, k, v, bias)


def _rel_bucket(rel):
    nb = _N_BUCKETS // 2
    max_exact = nb // 2
    ret = (rel > 0).astype(np.int32) * nb
    n = np.abs(rel)
    large = max_exact + (np.log(np.maximum(n, 1) / max_exact) / np.log(_MAX_DISTANCE / max_exact)
                         * (nb - max_exact)).astype(np.int32)
    large = np.minimum(large, nb - 1)
    return ret + np.where(n < max_exact, n, large)


def _band_bias(rel_bias, group):
    off = np.arange(-_HALF_KEYS, _HALF_KEYS + 1, dtype=np.int32) * _DILATIONS[group]
    heads = slice(group * _HEADS_PER_GROUP, (group + 1) * _HEADS_PER_GROUP)
    per_key = rel_bias[_rel_bucket(off)][:, heads].T.astype(_F32)
    qi = np.arange(_Q_BLOCK)[:, None]
    kk = np.arange(_K_WINDOW)[None, :]
    tables = []
    for shift in (_HALF_KEYS, 0, -_HALF_KEYS):
        j = kk - qi + shift
        valid = (j >= 0) & (j <= 2 * _HALF_KEYS)
        t = per_key[:, np.clip(j, 0, 2 * _HALF_KEYS)]
        tables.append(jnp.where(jnp.asarray(valid)[None], t, _NEG_INF))
    return jnp.stack(tables)


def _mix_body(x_ref, o1_ref, o4_ref, o16_ref, l1_ref, l4_ref, l16_ref,
              pre_ref, post_ref, wz_ref, wga_ref, wgb_ref, lng_ref, lnb_ref, ws_ref, bs_ref,
              watt_ref, wsgu_ref, wout_ref, y_ref, sgu_ref):
    x = x_ref[...]
    hb = _rms(x, pre_ref[...]).astype(_BF16)

    lses = (l1_ref[...], l4_ref[...], l16_ref[...])
    top = jnp.maximum(jnp.maximum(lses[0], lses[1]), lses[2])
    wts = [jnp.exp(l - top) for l in lses]
    att = (wts[0] * o1_ref[...] + wts[1] * o4_ref[...] + wts[2] * o16_ref[...]) / (
        wts[0] + wts[1] + wts[2])
    a = _dot(att.astype(_BF16), watt_ref[...])

    z = _dot(hb, wz_ref[...])
    z = 0.5 * z * (1.0 + lax.erf(z * (2.0 ** -0.5)))
    u, vv = z[:, :_SGU_W], z[:, _SGU_W:]
    mu = jnp.mean(vv, axis=-1, keepdims=True)
    var = jnp.mean(jnp.square(vv - mu), axis=-1, keepdims=True)
    vn = ((vv - mu) * lax.rsqrt(var + _EPS) * lng_ref[...] + lnb_ref[...]).astype(_BF16)
    for c in range(x.shape[0] // _CHUNK):
        rows = slice(c * _CHUNK, (c + 1) * _CHUNK)
        for g in range(_SGU_GROUPS):
            cols = slice(g * _LANES, (g + 1) * _LANES)
            mixed = _dot(ws_ref[g], vn[rows, cols]) + bs_ref[g]
            sgu_ref[rows, cols] = (u[rows, cols] * mixed).astype(_BF16)
    b = _dot(sgu_ref[...], wsgu_ref[...])

    merged = (jax.nn.sigmoid(_dot(hb, wga_ref[...])) * a
              + jax.nn.sigmoid(_dot(hb, wgb_ref[...])) * b)
    y = _dot(merged.astype(_BF16), wout_ref[...])
    y_ref[...] = x + _rms(y, post_ref[...])


def _mix(x, outs, lses, pre_g, post_g, wz, wga, wgb, ln_g, ln_b, ws, bs, watt, wsgu, wout):
    n = x.shape[0]
    tm = _TOKEN_TILE
    row = pl.BlockSpec((tm, _D), lambda i: (i, 0))
    grp = pl.BlockSpec((tm, _GROUP_W), lambda i: (i, 0))
    consts = (pre_g, post_g, wz, wga, wgb, ln_g, ln_b, ws, bs, watt, wsgu, wout)
    return pl.pallas_call(
        _mix_body,
        grid=(n // tm,),
        in_specs=[row] + [grp] * 6 + [_const_spec(c.shape) for c in consts],
        out_specs=row,
        out_shape=jax.ShapeDtypeStruct((n, _D), _F32),
        scratch_shapes=[pltpu.VMEM((tm, _SGU_W), _BF16)],
        compiler_params=_params(1),
        name="mix",
    )(x, *outs, *lses, *consts)


def _trunk(x, w):
    b, s, _ = x.shape
    n = b * s
    x1 = _ffn(x.reshape(n, _D), *w["ffn1"], final=False)
    q, k, v = _qkv(x1, w["mix_pre_g"], w["w_qkv"])

    outs, lses = [], []
    for g, d in enumerate(_DILATIONS):
        seq = s // d
        if d == 1:
            views = [a.reshape(b, 1, s, _QKV_W) for a in (q, k, v)]
            col = g
        else:
            views = [a.reshape(b, seq, d, len(_DILATIONS), _GROUP_W)[:, :, :, g].transpose(0, 2, 1, 3)
                     for a in (q, k, v)]
            col = 0
        o, lse = _att(*views, w["band_bias"][g], col)
        if d > 1:
            o, lse = (a.transpose(0, 2, 1, 3) for a in (o, lse))
        outs.append(o.reshape(n, _GROUP_W))
        lses.append(lse.reshape(n, _GROUP_W))

    x2 = _mix(x1, outs, lses, *w["mix"])
    y = _ffn(x2, *w["ffn2"], final=True)
    return y.reshape(b, s, _D)


def _row(v):
    return v.reshape(1, -1).astype(_F32)


def _ffn_weights(pre_g, post_g, fin_g, w_gate, w_up, w_down):
    wg = w_gate.astype(_BF16).reshape(_D, _N_FF_CHUNKS, _FF_CHUNK).transpose(1, 0, 2)
    wu = w_up.astype(_BF16).reshape(_D, _N_FF_CHUNKS, _FF_CHUNK).transpose(1, 0, 2)
    wd = w_down.astype(_BF16).reshape(_N_FF_CHUNKS, _FF_CHUNK, _D)
    return (_row(pre_g), _row(post_g), _row(fin_g), wg, wu, wd)


def kernel(x_prompt, x_sample, rel_bias, ffn1_pre_g, ffn1_post_g, ffn1_w_gate, ffn1_w_up, ffn1_w_down, mix_pre_g, w_in, sgu_ln_g, sgu_ln_b, sgu_w_s, sgu_b_s, w_att, w_sgu, w_out, mix_post_g, ffn2_pre_g, ffn2_post_g, ffn2_w_gate, ffn2_w_up, ffn2_w_down, final_g):
    assert ffn1_w_gate.shape[0] == 1, "one layer"
    l = 0
    w_in_b = w_in[l].astype(_BF16)
    z0 = 3 * _QKV_W
    w = {
        "ffn1": _ffn_weights(ffn1_pre_g[l], ffn1_post_g[l], final_g[l],
                             ffn1_w_gate[l], ffn1_w_up[l], ffn1_w_down[l]),
        "ffn2": _ffn_weights(ffn2_pre_g[l], ffn2_post_g[l], final_g[l],
                             ffn2_w_gate[l], ffn2_w_up[l], ffn2_w_down[l]),
        "mix_pre_g": _row(mix_pre_g[l]),
        "w_qkv": w_in_b[:, :z0],
        "band_bias": [_band_bias(rel_bias, g) for g in range(len(_DILATIONS))],
        "mix": (
            _row(mix_pre_g[l]), _row(mix_post_g[l]),
            w_in_b[:, z0:z0 + 2 * _SGU_W], w_in_b[:, z0 + 2 * _SGU_W:z0 + 2 * _SGU_W + _D],
            w_in_b[:, z0 + 2 * _SGU_W + _D:],
            _row(sgu_ln_g[l]), _row(sgu_ln_b[l]),
            sgu_w_s[l].astype(_BF16),
            jnp.broadcast_to(sgu_b_s[l].astype(_F32)[:, :, None], (_SGU_GROUPS, _CHUNK, _LANES)),
            w_att[l].astype(_BF16), w_sgu[l].astype(_BF16), w_out[l].astype(_BF16),
        ),
    }
    return (_trunk(x_prompt, w), _trunk(x_sample, w))
```

```python
import functools

import numpy as np
import jax
import jax.numpy as jnp
from jax import lax
from jax.experimental import pallas as pl
from jax.experimental.pallas import tpu as pltpu

_D = 1024
_DFF = 2816
_HEAD_DIM = 64
_HEADS_PER_GROUP = 4
_DILATIONS = (1, 4, 16)
_HALF_KEYS = 64
_GROUP_W = _HEADS_PER_GROUP * _HEAD_DIM
_QKV_W = len(_DILATIONS) * _GROUP_W
_SGU_W = 512
_SGU_GROUPS = 4
_CHUNK = 128
_N_BUCKETS = 32
_MAX_DISTANCE = 1024
_EPS = 1e-6
_NEG_INF = -1e30
_ATT_SCALE = _HEAD_DIM ** -0.5

_LANES = 128
_Q_BLOCK = 128
_K_WINDOW = _Q_BLOCK + 2 * _HALF_KEYS
_FF_CHUNK = 256
_N_FF_CHUNKS = _DFF // _FF_CHUNK
_TOKEN_TILE = 512
_VMEM_LIMIT = 56 * 1024 * 1024

_BF16 = jnp.bfloat16
_F32 = jnp.float32


def _rms(x, g):
    return x * lax.rsqrt(jnp.mean(x * x, axis=-1, keepdims=True) + _EPS) * g


def _dot(a, b):
    return jnp.dot(a, b, preferred_element_type=_F32)


def _const_spec(shape):
    return pl.BlockSpec(shape, lambda *_: (0,) * len(shape), pipeline_mode=pl.Buffered(1))


def _params(n_axes):
    return pltpu.CompilerParams(
        dimension_semantics=("arbitrary",) * n_axes, vmem_limit_bytes=_VMEM_LIMIT)


def _ffn_body(x_ref, pre_ref, post_ref, fin_ref, wg_ref, wu_ref, wd_ref, o_ref, xb_ref, acc_ref,
              *, final):
    x = x_ref[...]
    xb_ref[...] = _rms(x, pre_ref[...]).astype(_BF16)
    for c in range(_N_FF_CHUNKS):
        xb = xb_ref[...]
        gate = _dot(xb, wg_ref[c])
        up = _dot(xb, wu_ref[c])
        act = (gate * jax.nn.sigmoid(gate) * up).astype(_BF16)
        part = _dot(act, wd_ref[c])
        if c == 0:
            acc_ref[...] = part
        else:
            acc_ref[...] += part
    y = x + 0.5 * _rms(acc_ref[...], post_ref[...])
    if final:
        y = _rms(y, fin_ref[...])
    o_ref[...] = y


def _ffn(x, pre_g, post_g, fin_g, wg, wu, wd, *, final):
    n = x.shape[0]
    tm = _TOKEN_TILE
    row = pl.BlockSpec((tm, _D), lambda i: (i, 0))
    return pl.pallas_call(
        functools.partial(_ffn_body, final=final),
        grid=(n // tm,),
        in_specs=[row, _const_spec((1, _D)), _const_spec((1, _D)), _const_spec((1, _D)),
                  _const_spec(wg.shape), _const_spec(wu.shape), _const_spec(wd.shape)],
        out_specs=row,
        out_shape=jax.ShapeDtypeStruct((n, _D), _F32),
        scratch_shapes=[pltpu.VMEM((tm, _D), _BF16), pltpu.VMEM((tm, _D), _F32)],
        compiler_params=_params(1),
        name="ffn_final" if final else "ffn",
    )(x, pre_g, post_g, fin_g, wg, wu, wd)


def _qkv_body(x_ref, g_ref, w_ref, *refs):
    out_refs, stage_ref = refs[:-1], refs[-1]
    tm = x_ref.shape[0]
    hb = _rms(x_ref[...], g_ref[...]).astype(_BF16)
    slab = 0
    for part in range(3):
        for g, d in enumerate(_DILATIONS):
            col = part * _QKV_W + g * _GROUP_W
            res = _dot(hb, w_ref[:, col:col + _GROUP_W])
            if part == 0:
                res = res * _ATT_SCALE
            o_ref = out_refs[part * len(_DILATIONS) + g]
            if d == 1:
                o_ref[0] = res.astype(_BF16)
                continue
            for half in range(_GROUP_W // _LANES):
                lanes = slice(half * _LANES, (half + 1) * _LANES)
                stage_ref[slab] = res[:, lanes]
                for r in range(d):
                    o_ref[r, :, lanes] = stage_ref[slab, pl.ds(r, tm // d, stride=d), :].astype(_BF16)
                slab += 1


def _qkv(x, pre_g, w, batch, seq):
    tm = _TOKEN_TILE
    tiles = seq // tm
    n_stage = 3 * sum(d > 1 for d in _DILATIONS) * (_GROUP_W // _LANES)
    out_specs, out_shape = [], []
    for _ in range(3):
        for d in _DILATIONS:
            out_specs.append(pl.BlockSpec((None, d, tm // d, _GROUP_W), lambda b, j: (b, 0, j, 0)))
            out_shape.append(jax.ShapeDtypeStruct((batch, d, seq // d, _GROUP_W), _BF16))
    return pl.pallas_call(
        _qkv_body,
        grid=(batch, tiles),
        in_specs=[pl.BlockSpec((tm, _D), lambda b, j: (b * tiles + j, 0)), _const_spec((1, _D)),
                  _const_spec(w.shape)],
        out_specs=out_specs,
        out_shape=out_shape,
        scratch_shapes=[pltpu.VMEM((n_stage, tm, _LANES), _F32)],
        compiler_params=_params(2),
        name="qkv",
    )(x, pre_g, w)


def _att_body(q_ref, k_ref, v_ref, bias_ref, o_ref, lse_ref, *, seq):
    i = pl.program_id(2)
    start = jnp.clip(i * _Q_BLOCK - _HALF_KEYS, 0, seq - _K_WINDOW)
    start = pl.multiple_of(start, _HALF_KEYS)
    q = q_ref[...]
    kw = k_ref[pl.ds(start, _K_WINDOW), :]
    vw = v_ref[pl.ds(start, _K_WINDOW), :]
    low_head = lax.broadcasted_iota(jnp.int32, (_Q_BLOCK, _LANES), 1) < _HEAD_DIM
    for pair in range(_GROUP_W // _LANES):
        cols = slice(pair * _LANES, (pair + 1) * _LANES)
        q2, k2, v2 = q[:, cols], kw[:, cols], vw[:, cols]
        outs, lses = [], []
        for sub in range(2):
            keep = low_head if sub == 0 else jnp.logical_not(low_head)
            qh = jnp.where(keep, q2, jnp.zeros_like(q2))
            s = lax.dot_general(qh, k2, (((1,), (1,)), ((), ())), preferred_element_type=_F32)
            s = s + bias_ref[2 * pair + sub]
            m = jnp.max(s, axis=-1, keepdims=True)
            e = jnp.exp(s - m)
            den = jnp.sum(e, axis=-1, keepdims=True)
            outs.append(_dot(e.astype(_BF16), v2) / den)
            lses.append(m + jnp.log(den))
        o_ref[:, cols] = jnp.where(low_head, outs[0], outs[1])
        lse_ref[:, cols] = jnp.where(low_head, lses[0], lses[1])


def _att(q, k, v, bias):
    b, d, seq, _ = q.shape
    nb = seq // _Q_BLOCK
    assert seq % _Q_BLOCK == 0 and seq >= _K_WINDOW
    blk = pl.BlockSpec((None, None, _Q_BLOCK, _GROUP_W), lambda bi, r, i: (bi, r, i, 0))
    full = pl.BlockSpec((None, None, seq, _GROUP_W), lambda bi, r, i: (bi, r, 0, 0))

    def bias_map(bi, r, i):
        return (jnp.where(i == 0, 0, jnp.where(i == nb - 1, 2, 1)), 0, 0, 0)

    shape = jax.ShapeDtypeStruct((b, d, seq, _GROUP_W), _F32)
    return pl.pallas_call(
        functools.partial(_att_body, seq=seq),
        grid=(b, d, nb),
        in_specs=[blk, full, full,
                  pl.BlockSpec((None, _HEADS_PER_GROUP, _Q_BLOCK, _K_WINDOW), bias_map)],
        out_specs=[blk, blk],
        out_shape=[shape, shape],
        compiler_params=_params(3),
        name=f"att_d{d}",
    )(q, k, v, bias)


def _rel_bucket(rel):
    nb = _N_BUCKETS // 2
    max_exact = nb // 2
    ret = (rel > 0).astype(np.int32) * nb
    n = np.abs(rel)
    large = max_exact + (np.log(np.maximum(n, 1) / max_exact) / np.log(_MAX_DISTANCE / max_exact)
                         * (nb - max_exact)).astype(np.int32)
    large = np.minimum(large, nb - 1)
    return ret + np.where(n < max_exact, n, large)


def _band_bias(rel_bias, group):
    off = np.arange(-_HALF_KEYS, _HALF_KEYS + 1, dtype=np.int32) * _DILATIONS[group]
    heads = slice(group * _HEADS_PER_GROUP, (group + 1) * _HEADS_PER_GROUP)
    per_key = rel_bias[_rel_bucket(off)][:, heads].T.astype(_F32)
    qi = np.arange(_Q_BLOCK)[:, None]
    kk = np.arange(_K_WINDOW)[None, :]
    tables = []
    for shift in (_HALF_KEYS, 0, -_HALF_KEYS):
        j = kk - qi + shift
        valid = (j >= 0) & (j <= 2 * _HALF_KEYS)
        t = per_key[:, np.clip(j, 0, 2 * _HALF_KEYS)]
        tables.append(jnp.where(jnp.asarray(valid)[None], t, _NEG_INF))
    return jnp.stack(tables)


def _mix_body(x_ref, o1_ref, o4_ref, o16_ref, l1_ref, l4_ref, l16_ref,
              pre_ref, post_ref, wz_ref, wga_ref, wgb_ref, lng_ref, lnb_ref, ws_ref, bs_ref,
              watt_ref, wsgu_ref, wout_ref, y_ref, sgu_ref, stage_ref):
    x = x_ref[...]
    tm = x.shape[0]
    hb = _rms(x, pre_ref[...]).astype(_BF16)

    def token_order(ref, d, slab, lanes):
        if d == 1:
            return ref[0, :, lanes]
        for r in range(d):
            stage_ref[slab, pl.ds(r, tm // d, stride=d), :] = ref[r, :, lanes]
        return stage_ref[slab]

    att_halves = []
    for half in range(_GROUP_W // _LANES):
        lanes = slice(half * _LANES, (half + 1) * _LANES)
        slab = 4 * half
        outs = (token_order(o1_ref, 1, 0, lanes), token_order(o4_ref, 4, slab, lanes),
                token_order(o16_ref, 16, slab + 1, lanes))
        lses = (token_order(l1_ref, 1, 0, lanes), token_order(l4_ref, 4, slab + 2, lanes),
                token_order(l16_ref, 16, slab + 3, lanes))
        top = jnp.maximum(jnp.maximum(lses[0], lses[1]), lses[2])
        wts = [jnp.exp(l - top) for l in lses]
        att_halves.append((wts[0] * outs[0] + wts[1] * outs[1] + wts[2] * outs[2])
                          / (wts[0] + wts[1] + wts[2]))
    att = jnp.concatenate(att_halves, axis=1)
    a = _dot(att.astype(_BF16), watt_ref[...])

    z = _dot(hb, wz_ref[...])
    z = 0.5 * z * (1.0 + lax.erf(z * (2.0 ** -0.5)))
    u, vv = z[:, :_SGU_W], z[:, _SGU_W:]
    mu = jnp.mean(vv, axis=-1, keepdims=True)
    var = jnp.mean(jnp.square(vv - mu), axis=-1, keepdims=True)
    vn = ((vv - mu) * lax.rsqrt(var + _EPS) * lng_ref[...] + lnb_ref[...]).astype(_BF16)
    for c in range(tm // _CHUNK):
        rows = slice(c * _CHUNK, (c + 1) * _CHUNK)
        for g in range(_SGU_GROUPS):
            cols = slice(g * _LANES, (g + 1) * _LANES)
            mixed = _dot(ws_ref[g], vn[rows, cols]) + bs_ref[g]
            sgu_ref[rows, cols] = (u[rows, cols] * mixed).astype(_BF16)
    b = _dot(sgu_ref[...], wsgu_ref[...])

    merged = (jax.nn.sigmoid(_dot(hb, wga_ref[...])) * a
              + jax.nn.sigmoid(_dot(hb, wgb_ref[...])) * b)
    y = _dot(merged.astype(_BF16), wout_ref[...])
    y_ref[...] = x + _rms(y, post_ref[...])


def _mix(x, outs, lses, batch, seq,
         pre_g, post_g, wz, wga, wgb, ln_g, ln_b, ws, bs, watt, wsgu, wout):
    tm = _TOKEN_TILE
    tiles = seq // tm
    row = pl.BlockSpec((tm, _D), lambda b, j: (b * tiles + j, 0))
    grp = [pl.BlockSpec((None, d, tm // d, _GROUP_W), lambda b, j: (b, 0, j, 0)) for d in _DILATIONS]
    consts = (pre_g, post_g, wz, wga, wgb, ln_g, ln_b, ws, bs, watt, wsgu, wout)
    n_stage = 2 * sum(d > 1 for d in _DILATIONS) * (_GROUP_W // _LANES)
    return pl.pallas_call(
        _mix_body,
        grid=(batch, tiles),
        in_specs=[row] + grp + grp + [_const_spec(c.shape) for c in consts],
        out_specs=row,
        out_shape=jax.ShapeDtypeStruct((batch * seq, _D), _F32),
        scratch_shapes=[pltpu.VMEM((tm, _SGU_W), _BF16), pltpu.VMEM((n_stage, tm, _LANES), _F32)],
        compiler_params=_params(2),
        name="mix",
    )(x, *outs, *lses, *consts)


def _trunk(x, w):
    b, s, _ = x.shape
    n = b * s
    x1 = _ffn(x.reshape(n, _D), *w["ffn1"], final=False)
    qkv = _qkv(x1, w["mix_pre_g"], w["w_qkv"], b, s)
    ng = len(_DILATIONS)
    outs, lses = [], []
    for g in range(ng):
        o, lse = _att(qkv[g], qkv[ng + g], qkv[2 * ng + g], w["band_bias"][g])
        outs.append(o)
        lses.append(lse)
    x2 = _mix(x1, outs, lses, b, s, *w["mix"])
    y = _ffn(x2, *w["ffn2"], final=True)
    return y.reshape(b, s, _D)


def _row(v):
    return v.reshape(1, -1).astype(_F32)


def _ffn_weights(pre_g, post_g, fin_g, w_gate, w_up, w_down):
    wg = w_gate.astype(_BF16).reshape(_D, _N_FF_CHUNKS, _FF_CHUNK).transpose(1, 0, 2)
    wu = w_up.astype(_BF16).reshape(_D, _N_FF_CHUNKS, _FF_CHUNK).transpose(1, 0, 2)
    wd = w_down.astype(_BF16).reshape(_N_FF_CHUNKS, _FF_CHUNK, _D)
    return (_row(pre_g), _row(post_g), _row(fin_g), wg, wu, wd)


def kernel(x_prompt, x_sample, rel_bias, ffn1_pre_g, ffn1_post_g, ffn1_w_gate, ffn1_w_up, ffn1_w_down, mix_pre_g, w_in, sgu_ln_g, sgu_ln_b, sgu_w_s, sgu_b_s, w_att, w_sgu, w_out, mix_post_g, ffn2_pre_g, ffn2_post_g, ffn2_w_gate, ffn2_w_up, ffn2_w_down, final_g):
    assert ffn1_w_gate.shape[0] == 1, "one layer"
    l = 0
    w_in_b = w_in[l].astype(_BF16)
    z0 = 3 * _QKV_W
    w = {
        "ffn1": _ffn_weights(ffn1_pre_g[l], ffn1_post_g[l], final_g[l],
                             ffn1_w_gate[l], ffn1_w_up[l], ffn1_w_down[l]),
        "ffn2": _ffn_weights(ffn2_pre_g[l], ffn2_post_g[l], final_g[l],
                             ffn2_w_gate[l], ffn2_w_up[l], ffn2_w_down[l]),
        "mix_pre_g": _row(mix_pre_g[l]),
        "w_qkv": w_in_b[:, :z0],
        "band_bias": [_band_bias(rel_bias, g) for g in range(len(_DILATIONS))],
        "mix": (
            _row(mix_pre_g[l]), _row(mix_post_g[l]),
            w_in_b[:, z0:z0 + 2 * _SGU_W], w_in_b[:, z0 + 2 * _SGU_W:z0 + 2 * _SGU_W + _D],
            w_in_b[:, z0 + 2 * _SGU_W + _D:],
            _row(sgu_ln_g[l]), _row(sgu_ln_b[l]),
            sgu_w_s[l].astype(_BF16),
            jnp.broadcast_to(sgu_b_s[l].astype(_F32)[:, :, None], (_SGU_GROUPS, _CHUNK, _LANES)),
            w_att[l].astype(_BF16), w_sgu[l].astype(_BF16), w_out[l].astype(_BF16),
        ),
    }
    return (_trunk(x_prompt, w), _trunk(x_sample, w))
```

```python
import functools

import numpy as np
import jax
import jax.numpy as jnp
from jax import lax
from jax.experimental import pallas as pl
from jax.experimental.pallas import tpu as pltpu

_D = 1024
_DFF = 2816
_HEAD_DIM = 64
_HEADS_PER_GROUP = 4
_DILATIONS = (1, 4, 16)
_HALF_KEYS = 64
_GROUP_W = _HEADS_PER_GROUP * _HEAD_DIM
_QKV_W = len(_DILATIONS) * _GROUP_W
_SGU_W = 512
_SGU_GROUPS = 4
_CHUNK = 128
_N_BUCKETS = 32
_MAX_DISTANCE = 1024
_EPS = 1e-6
_NEG_INF = -1e30
_ATT_SCALE = _HEAD_DIM ** -0.5

_LANES = 128
_Q_BLOCK = 128
_K_WINDOW = _Q_BLOCK + 2 * _HALF_KEYS
_ATT_BLOCKS_PER_STEP = 4
_FF_CHUNK = 256
_N_FF_CHUNKS = _DFF // _FF_CHUNK
_TOKEN_TILE = 512
_VMEM_LIMIT = 56 * 1024 * 1024

_BF16 = jnp.bfloat16
_F32 = jnp.float32


def _rms(x, g):
    return x * lax.rsqrt(jnp.mean(x * x, axis=-1, keepdims=True) + _EPS) * g


def _dot(a, b):
    return jnp.dot(a, b, preferred_element_type=_F32)


def _const_spec(shape):
    return pl.BlockSpec(shape, lambda *_: (0,) * len(shape), pipeline_mode=pl.Buffered(1))


def _params(n_axes):
    return pltpu.CompilerParams(
        dimension_semantics=("arbitrary",) * n_axes, vmem_limit_bytes=_VMEM_LIMIT)


def _ffn_body(x_ref, pre_ref, post_ref, fin_ref, wg_ref, wu_ref, wd_ref, o_ref, xb_ref, acc_ref,
              *, final):
    x = x_ref[...]
    xb_ref[...] = _rms(x, pre_ref[...]).astype(_BF16)
    for c in range(_N_FF_CHUNKS):
        xb = xb_ref[...]
        gate = _dot(xb, wg_ref[c])
        up = _dot(xb, wu_ref[c])
        act = (gate * jax.nn.sigmoid(gate) * up).astype(_BF16)
        part = _dot(act, wd_ref[c])
        if c == 0:
            acc_ref[...] = part
        else:
            acc_ref[...] += part
    y = x + 0.5 * _rms(acc_ref[...], post_ref[...])
    if final:
        y = _rms(y, fin_ref[...])
    o_ref[...] = y


def _ffn(x, pre_g, post_g, fin_g, wg, wu, wd, *, final):
    n = x.shape[0]
    tm = _TOKEN_TILE
    row = pl.BlockSpec((tm, _D), lambda i: (i, 0))
    return pl.pallas_call(
        functools.partial(_ffn_body, final=final),
        grid=(n // tm,),
        in_specs=[row, _const_spec((1, _D)), _const_spec((1, _D)), _const_spec((1, _D)),
                  _const_spec(wg.shape), _const_spec(wu.shape), _const_spec(wd.shape)],
        out_specs=row,
        out_shape=jax.ShapeDtypeStruct((n, _D), _F32),
        scratch_shapes=[pltpu.VMEM((tm, _D), _BF16), pltpu.VMEM((tm, _D), _F32)],
        compiler_params=_params(1),
        name="ffn_final" if final else "ffn",
    )(x, pre_g, post_g, fin_g, wg, wu, wd)


def _qkv_body(x_ref, g_ref, w_ref, *refs):
    out_refs, stage_ref = refs[:-1], refs[-1]
    tm = x_ref.shape[0]
    hb = _rms(x_ref[...], g_ref[...]).astype(_BF16)
    slab = 0
    for part in range(3):
        for g, d in enumerate(_DILATIONS):
            col = part * _QKV_W + g * _GROUP_W
            res = _dot(hb, w_ref[:, col:col + _GROUP_W])
            if part == 0:
                res = res * _ATT_SCALE
            o_ref = out_refs[part * len(_DILATIONS) + g]
            if d == 1:
                o_ref[0] = res.astype(_BF16)
                continue
            for half in range(_GROUP_W // _LANES):
                lanes = slice(half * _LANES, (half + 1) * _LANES)
                stage_ref[slab] = res[:, lanes]
                for r in range(d):
                    o_ref[r, :, lanes] = stage_ref[slab, pl.ds(r, tm // d, stride=d), :].astype(_BF16)
                slab += 1


def _qkv(x, pre_g, w, batch, seq):
    tm = _TOKEN_TILE
    tiles = seq // tm
    n_stage = 3 * sum(d > 1 for d in _DILATIONS) * (_GROUP_W // _LANES)
    out_specs, out_shape = [], []
    for _ in range(3):
        for d in _DILATIONS:
            out_specs.append(pl.BlockSpec((None, d, tm // d, _GROUP_W), lambda b, j: (b, 0, j, 0)))
            out_shape.append(jax.ShapeDtypeStruct((batch, d, seq // d, _GROUP_W), _BF16))
    return pl.pallas_call(
        _qkv_body,
        grid=(batch, tiles),
        in_specs=[pl.BlockSpec((tm, _D), lambda b, j: (b * tiles + j, 0)), _const_spec((1, _D)),
                  _const_spec(w.shape)],
        out_specs=out_specs,
        out_shape=out_shape,
        scratch_shapes=[pltpu.VMEM((n_stage, tm, _LANES), _F32)],
        compiler_params=_params(2),
        name="qkv",
    )(x, pre_g, w)


def _att_body(q_ref, k_ref, v_ref, bias_ref, o_ref, lse_ref, *, seq, blocks):
    step = pl.program_id(2)
    last = seq // _Q_BLOCK - 1
    low_head = lax.broadcasted_iota(jnp.int32, (_Q_BLOCK, _LANES), 1) < _HEAD_DIM
    for blk in range(blocks):
        i = step * blocks + blk
        start = jnp.clip(i * _Q_BLOCK - _HALF_KEYS, 0, seq - _K_WINDOW)
        start = pl.multiple_of(start, _HALF_KEYS)
        variant = jnp.where(i == 0, 0, jnp.where(i == last, 2, 1))
        rows = slice(blk * _Q_BLOCK, (blk + 1) * _Q_BLOCK)
        for pair in range(_GROUP_W // _LANES):
            cols = slice(pair * _LANES, (pair + 1) * _LANES)
            q2 = q_ref[rows, cols]
            k2 = k_ref[pl.ds(start, _K_WINDOW), cols]
            v2 = v_ref[pl.ds(start, _K_WINDOW), cols]
            outs, lses = [], []
            for sub in range(2):
                keep = low_head if sub == 0 else jnp.logical_not(low_head)
                qh = jnp.where(keep, q2, jnp.zeros_like(q2))
                s = lax.dot_general(qh, k2, (((1,), (1,)), ((), ())), preferred_element_type=_F32)
                s = s + bias_ref[variant, 2 * pair + sub]
                m = jnp.max(s, axis=-1, keepdims=True)
                e = jnp.exp(s - m)
                den = jnp.sum(e, axis=-1, keepdims=True)
                outs.append(_dot(e.astype(_BF16), v2) / den)
                lses.append(m + jnp.log(den))
            o_ref[rows, cols] = jnp.where(low_head, outs[0], outs[1])
            lse_ref[rows, cols] = jnp.where(low_head, lses[0], lses[1])


def _att(q, k, v, bias):
    b, d, seq, _ = q.shape
    nb = seq // _Q_BLOCK
    assert seq % _Q_BLOCK == 0 and seq >= _K_WINDOW
    blocks = min(_ATT_BLOCKS_PER_STEP, nb)
    assert nb % blocks == 0
    blk = pl.BlockSpec((None, None, blocks * _Q_BLOCK, _GROUP_W), lambda bi, r, i: (bi, r, i, 0))
    full = pl.BlockSpec((None, None, seq, _GROUP_W), lambda bi, r, i: (bi, r, 0, 0))
    shape = jax.ShapeDtypeStruct((b, d, seq, _GROUP_W), _F32)
    return pl.pallas_call(
        functools.partial(_att_body, seq=seq, blocks=blocks),
        grid=(b, d, nb // blocks),
        in_specs=[blk, full, full, _const_spec(bias.shape)],
        out_specs=[blk, blk],
        out_shape=[shape, shape],
        compiler_params=_params(3),
        name=f"att_d{d}",
    )(q, k, v, bias)


def _rel_bucket(rel):
    nb = _N_BUCKETS // 2
    max_exact = nb // 2
    ret = (rel > 0).astype(np.int32) * nb
    n = np.abs(rel)
    large = max_exact + (np.log(np.maximum(n, 1) / max_exact) / np.log(_MAX_DISTANCE / max_exact)
                         * (nb - max_exact)).astype(np.int32)
    large = np.minimum(large, nb - 1)
    return ret + np.where(n < max_exact, n, large)


def _band_bias(rel_bias, group):
    off = np.arange(-_HALF_KEYS, _HALF_KEYS + 1, dtype=np.int32) * _DILATIONS[group]
    heads = slice(group * _HEADS_PER_GROUP, (group + 1) * _HEADS_PER_GROUP)
    per_key = rel_bias[_rel_bucket(off)][:, heads].T.astype(_F32)
    n_keys = 2 * _HALF_KEYS + 1
    period = _K_WINDOW + _Q_BLOCK
    tables = []
    for shift in (_HALF_KEYS, 0, -_HALF_KEYS):
        lead = _Q_BLOCK - 1 - shift
        v = jnp.pad(per_key, ((0, 0), (lead, period - lead - n_keys)), constant_values=_NEG_INF)
        skew = jnp.tile(v, (1, _Q_BLOCK))[:, :_Q_BLOCK * (period - 1)]
        skew = skew.reshape(_HEADS_PER_GROUP, _Q_BLOCK, period - 1)
        tables.append(skew[:, :, _Q_BLOCK - 1:])
    return jnp.stack(tables)


def _mix_body(x_ref, o1_ref, o4_ref, o16_ref, l1_ref, l4_ref, l16_ref,
              pre_ref, post_ref, wz_ref, wga_ref, wgb_ref, lng_ref, lnb_ref, ws_ref, bs_ref,
              watt_ref, wsgu_ref, wout_ref, y_ref, sgu_ref, stage_ref):
    x = x_ref[...]
    tm = x.shape[0]
    hb = _rms(x, pre_ref[...]).astype(_BF16)

    def token_order(ref, d, slab, lanes):
        if d == 1:
            return ref[0, :, lanes]
        for r in range(d):
            stage_ref[slab, pl.ds(r, tm // d, stride=d), :] = ref[r, :, lanes]
        return stage_ref[slab]

    att_halves = []
    for half in range(_GROUP_W // _LANES):
        lanes = slice(half * _LANES, (half + 1) * _LANES)
        slab = 4 * half
        outs = (token_order(o1_ref, 1, 0, lanes), token_order(o4_ref, 4, slab, lanes),
                token_order(o16_ref, 16, slab + 1, lanes))
        lses = (token_order(l1_ref, 1, 0, lanes), token_order(l4_ref, 4, slab + 2, lanes),
                token_order(l16_ref, 16, slab + 3, lanes))
        top = jnp.maximum(jnp.maximum(lses[0], lses[1]), lses[2])
        wts = [jnp.exp(l - top) for l in lses]
        att_halves.append((wts[0] * outs[0] + wts[1] * outs[1] + wts[2] * outs[2])
                          / (wts[0] + wts[1] + wts[2]))
    att = jnp.concatenate(att_halves, axis=1)
    a = _dot(att.astype(_BF16), watt_ref[...])

    z = _dot(hb, wz_ref[...])
    z = 0.5 * z * (1.0 + lax.erf(z * (2.0 ** -0.5)))
    u, vv = z[:, :_SGU_W], z[:, _SGU_W:]
    mu = jnp.mean(vv, axis=-1, keepdims=True)
    var = jnp.mean(jnp.square(vv - mu), axis=-1, keepdims=True)
    vn = ((vv - mu) * lax.rsqrt(var + _EPS) * lng_ref[...] + lnb_ref[...]).astype(_BF16)
    for c in range(tm // _CHUNK):
        rows = slice(c * _CHUNK, (c + 1) * _CHUNK)
        for g in range(_SGU_GROUPS):
            cols = slice(g * _LANES, (g + 1) * _LANES)
            mixed = _dot(ws_ref[g], vn[rows, cols]) + bs_ref[g]
            sgu_ref[rows, cols] = (u[rows, cols] * mixed).astype(_BF16)
    b = _dot(sgu_ref[...], wsgu_ref[...])

    merged = (jax.nn.sigmoid(_dot(hb, wga_ref[...])) * a
              + jax.nn.sigmoid(_dot(hb, wgb_ref[...])) * b)
    y = _dot(merged.astype(_BF16), wout_ref[...])
    y_ref[...] = x + _rms(y, post_ref[...])


def _mix(x, outs, lses, batch, seq,
         pre_g, post_g, wz, wga, wgb, ln_g, ln_b, ws, bs, watt, wsgu, wout):
    tm = _TOKEN_TILE
    tiles = seq // tm
    row = pl.BlockSpec((tm, _D), lambda b, j: (b * tiles + j, 0))
    grp = [pl.BlockSpec((None, d, tm // d, _GROUP_W), lambda b, j: (b, 0, j, 0)) for d in _DILATIONS]
    consts = (pre_g, post_g, wz, wga, wgb, ln_g, ln_b, ws, bs, watt, wsgu, wout)
    n_stage = 2 * sum(d > 1 for d in _DILATIONS) * (_GROUP_W // _LANES)
    return pl.pallas_call(
        _mix_body,
        grid=(batch, tiles),
        in_specs=[row] + grp + grp + [_const_spec(c.shape) for c in consts],
        out_specs=row,
        out_shape=jax.ShapeDtypeStruct((batch * seq, _D), _F32),
        scratch_shapes=[pltpu.VMEM((tm, _SGU_W), _BF16), pltpu.VMEM((n_stage, tm, _LANES), _F32)],
        compiler_params=_params(2),
        name="mix",
    )(x, *outs, *lses, *consts)


def _trunk(x, w):
    b, s, _ = x.shape
    n = b * s
    x1 = _ffn(x.reshape(n, _D), *w["ffn1"], final=False)
    qkv = _qkv(x1, w["mix_pre_g"], w["w_qkv"], b, s)
    ng = len(_DILATIONS)
    outs, lses = [], []
    for g in range(ng):
        o, lse = _att(qkv[g], qkv[ng + g], qkv[2 * ng + g], w["band_bias"][g])
        outs.append(o)
        lses.append(lse)
    x2 = _mix(x1, outs, lses, b, s, *w["mix"])
    y = _ffn(x2, *w["ffn2"], final=True)
    return y.reshape(b, s, _D)


def _row(v):
    return v.reshape(1, -1).astype(_F32)


def _ffn_weights(pre_g, post_g, fin_g, w_gate, w_up, w_down):
    wg = w_gate.astype(_BF16).reshape(_D, _N_FF_CHUNKS, _FF_CHUNK).transpose(1, 0, 2)
    wu = w_up.astype(_BF16).reshape(_D, _N_FF_CHUNKS, _FF_CHUNK).transpose(1, 0, 2)
    wd = w_down.astype(_BF16).reshape(_N_FF_CHUNKS, _FF_CHUNK, _D)
    return (_row(pre_g), _row(post_g), _row(fin_g), wg, wu, wd)


def kernel(x_prompt, x_sample, rel_bias, ffn1_pre_g, ffn1_post_g, ffn1_w_gate, ffn1_w_up, ffn1_w_down, mix_pre_g, w_in, sgu_ln_g, sgu_ln_b, sgu_w_s, sgu_b_s, w_att, w_sgu, w_out, mix_post_g, ffn2_pre_g, ffn2_post_g, ffn2_w_gate, ffn2_w_up, ffn2_w_down, final_g):
    assert ffn1_w_gate.shape[0] == 1, "one layer"
    l = 0
    w_in_b = w_in[l].astype(_BF16)
    z0 = 3 * _QKV_W
    w = {
        "ffn1": _ffn_weights(ffn1_pre_g[l], ffn1_post_g[l], final_g[l],
                             ffn1_w_gate[l], ffn1_w_up[l], ffn1_w_down[l]),
        "ffn2": _ffn_weights(ffn2_pre_g[l], ffn2_post_g[l], final_g[l],
                             ffn2_w_gate[l], ffn2_w_up[l], ffn2_w_down[l]),
        "mix_pre_g": _row(mix_pre_g[l]),
        "w_qkv": w_in_b[:, :z0],
        "band_bias": [_band_bias(rel_bias, g) for g in range(len(_DILATIONS))],
        "mix": (
            _row(mix_pre_g[l]), _row(mix_post_g[l]),
            w_in_b[:, z0:z0 + 2 * _SGU_W], w_in_b[:, z0 + 2 * _SGU_W:z0 + 2 * _SGU_W + _D],
            w_in_b[:, z0 + 2 * _SGU_W + _D:],
            _row(sgu_ln_g[l]), _row(sgu_ln_b[l]),
            sgu_w_s[l].astype(_BF16),
            jnp.broadcast_to(sgu_b_s[l].astype(_F32)[:, :, None], (_SGU_GROUPS, _CHUNK, _LANES)),
            w_att[l].astype(_BF16), w_sgu[l].astype(_BF16), w_out[l].astype(_BF16),
        ),
    }
    return (_trunk(x_prompt, w), _trunk(x_sample, w))
```

```python
import functools

import numpy as np
import jax
import jax.numpy as jnp
from jax import lax
from jax.experimental import pallas as pl
from jax.experimental.pallas import tpu as pltpu

_D = 1024
_DFF = 2816
_HEAD_DIM = 64
_HEADS_PER_GROUP = 4
_DILATIONS = (1, 4, 16)
_HALF_KEYS = 64
_GROUP_W = _HEADS_PER_GROUP * _HEAD_DIM
_QKV_W = len(_DILATIONS) * _GROUP_W
_SGU_W = 512
_SGU_GROUPS = 4
_CHUNK = 128
_N_BUCKETS = 32
_MAX_DISTANCE = 1024
_EPS = 1e-6
_NEG_INF = -1e30
_ATT_SCALE = _HEAD_DIM ** -0.5

_LANES = 128
_Q_BLOCK = 128
_K_WINDOW = _Q_BLOCK + 2 * _HALF_KEYS
_ATT_BLOCKS_PER_STEP = 4
_FF_CHUNK = 256
_N_FF_CHUNKS = _DFF // _FF_CHUNK
_TOKEN_TILE = 1024
_VMEM_LIMIT = 56 * 1024 * 1024

_BF16 = jnp.bfloat16
_F32 = jnp.float32


def _rms(x, g):
    return x * lax.rsqrt(jnp.mean(x * x, axis=-1, keepdims=True) + _EPS) * g


def _dot(a, b):
    return jnp.dot(a, b, preferred_element_type=_F32)


def _const_spec(shape):
    return pl.BlockSpec(shape, lambda *_: (0,) * len(shape), pipeline_mode=pl.Buffered(1))


def _params(n_axes):
    return pltpu.CompilerParams(
        dimension_semantics=("arbitrary",) * n_axes, vmem_limit_bytes=_VMEM_LIMIT)


def _ffn_body(x_ref, pre_ref, post_ref, fin_ref, wg_ref, wu_ref, wd_ref, o_ref, xb_ref, acc_ref,
              *, final):
    x = x_ref[...]
    xb_ref[...] = _rms(x, pre_ref[...]).astype(_BF16)
    for c in range(_N_FF_CHUNKS):
        xb = xb_ref[...]
        ff = slice(c * _FF_CHUNK, (c + 1) * _FF_CHUNK)
        gate = _dot(xb, wg_ref[:, ff])
        up = _dot(xb, wu_ref[:, ff])
        act = (gate * jax.nn.sigmoid(gate) * up).astype(_BF16)
        part = _dot(act, wd_ref[ff, :])
        if c == 0:
            acc_ref[...] = part
        else:
            acc_ref[...] += part
    y = x + 0.5 * _rms(acc_ref[...], post_ref[...])
    if final:
        y = _rms(y, fin_ref[...])
    o_ref[...] = y


def _ffn(x, pre_g, post_g, fin_g, wg, wu, wd, *, final):
    n = x.shape[0]
    tm = _TOKEN_TILE
    row = pl.BlockSpec((tm, _D), lambda i: (i, 0))
    return pl.pallas_call(
        functools.partial(_ffn_body, final=final),
        grid=(n // tm,),
        in_specs=[row, _const_spec((1, _D)), _const_spec((1, _D)), _const_spec((1, _D)),
                  _const_spec(wg.shape), _const_spec(wu.shape), _const_spec(wd.shape)],
        out_specs=row,
        out_shape=jax.ShapeDtypeStruct((n, _D), _F32),
        scratch_shapes=[pltpu.VMEM((tm, _D), _BF16), pltpu.VMEM((tm, _D), _F32)],
        compiler_params=_params(1),
        name="ffn_final" if final else "ffn",
    )(x, pre_g, post_g, fin_g, wg, wu, wd)


def _qkv_body(x_ref, g_ref, w_ref, *refs):
    out_refs, stage_ref = refs[:-1], refs[-1]
    tm = x_ref.shape[0]
    hb = _rms(x_ref[...], g_ref[...]).astype(_BF16)
    slab = 0
    for part in range(3):
        for g, d in enumerate(_DILATIONS):
            col = part * _QKV_W + g * _GROUP_W
            res = _dot(hb, w_ref[:, col:col + _GROUP_W])
            if part == 0:
                res = res * _ATT_SCALE
            o_ref = out_refs[part * len(_DILATIONS) + g]
            if d == 1:
                o_ref[0] = res.astype(_BF16)
                continue
            for half in range(_GROUP_W // _LANES):
                lanes = slice(half * _LANES, (half + 1) * _LANES)
                stage_ref[slab] = res[:, lanes]
                for r in range(d):
                    o_ref[r, :, lanes] = stage_ref[slab, pl.ds(r, tm // d, stride=d), :].astype(_BF16)
                slab += 1


def _qkv(x, pre_g, w, batch, seq):
    tm = _TOKEN_TILE
    tiles = seq // tm
    n_stage = 3 * sum(d > 1 for d in _DILATIONS) * (_GROUP_W // _LANES)
    out_specs, out_shape = [], []
    for _ in range(3):
        for d in _DILATIONS:
            out_specs.append(pl.BlockSpec((None, d, tm // d, _GROUP_W), lambda b, j: (b, 0, j, 0)))
            out_shape.append(jax.ShapeDtypeStruct((batch, d, seq // d, _GROUP_W), _BF16))
    return pl.pallas_call(
        _qkv_body,
        grid=(batch, tiles),
        in_specs=[pl.BlockSpec((tm, _D), lambda b, j: (b * tiles + j, 0)), _const_spec((1, _D)),
                  _const_spec(w.shape)],
        out_specs=out_specs,
        out_shape=out_shape,
        scratch_shapes=[pltpu.VMEM((n_stage, tm, _LANES), _F32)],
        compiler_params=_params(2),
        name="qkv",
    )(x, pre_g, w)


def _att_body(q_ref, k_ref, v_ref, bias_ref, o_ref, lse_ref, *, seq, blocks):
    step = pl.program_id(2)
    last = seq // _Q_BLOCK - 1
    low_head = lax.broadcasted_iota(jnp.int32, (_Q_BLOCK, _LANES), 1) < _HEAD_DIM
    for blk in range(blocks):
        i = step * blocks + blk
        start = jnp.clip(i * _Q_BLOCK - _HALF_KEYS, 0, seq - _K_WINDOW)
        start = pl.multiple_of(start, _HALF_KEYS)
        variant = jnp.where(i == 0, 0, jnp.where(i == last, 2, 1))
        rows = slice(blk * _Q_BLOCK, (blk + 1) * _Q_BLOCK)
        for pair in range(_GROUP_W // _LANES):
            cols = slice(pair * _LANES, (pair + 1) * _LANES)
            q2 = q_ref[rows, cols]
            k2 = k_ref[pl.ds(start, _K_WINDOW), cols]
            v2 = v_ref[pl.ds(start, _K_WINDOW), cols]
            outs, lses = [], []
            for sub in range(2):
                keep = low_head if sub == 0 else jnp.logical_not(low_head)
                qh = jnp.where(keep, q2, jnp.zeros_like(q2))
                s = lax.dot_general(qh, k2, (((1,), (1,)), ((), ())), preferred_element_type=_F32)
                s = s + bias_ref[variant, 2 * pair + sub]
                m = jnp.max(s, axis=-1, keepdims=True)
                e = jnp.exp(s - m)
                den = jnp.sum(e, axis=-1, keepdims=True)
                outs.append(_dot(e.astype(_BF16), v2) / den)
                lses.append(m + jnp.log(den))
            o_ref[rows, cols] = jnp.where(low_head, outs[0], outs[1])
            lse_ref[rows, cols] = jnp.where(low_head, lses[0], lses[1])


def _att(q, k, v, bias):
    b, d, seq, _ = q.shape
    nb = seq // _Q_BLOCK
    assert seq % _Q_BLOCK == 0 and seq >= _K_WINDOW
    blocks = min(_ATT_BLOCKS_PER_STEP, nb)
    assert nb % blocks == 0
    blk = pl.BlockSpec((None, None, blocks * _Q_BLOCK, _GROUP_W), lambda bi, r, i: (bi, r, i, 0))
    full = pl.BlockSpec((None, None, seq, _GROUP_W), lambda bi, r, i: (bi, r, 0, 0))
    shape = jax.ShapeDtypeStruct((b, d, seq, _GROUP_W), _F32)
    return pl.pallas_call(
        functools.partial(_att_body, seq=seq, blocks=blocks),
        grid=(b, d, nb // blocks),
        in_specs=[blk, full, full, _const_spec(bias.shape)],
        out_specs=[blk, blk],
        out_shape=[shape, shape],
        compiler_params=_params(3),
        name=f"att_d{d}",
    )(q, k, v, bias)


def _rel_bucket(rel):
    nb = _N_BUCKETS // 2
    max_exact = nb // 2
    ret = (rel > 0).astype(np.int32) * nb
    n = np.abs(rel)
    large = max_exact + (np.log(np.maximum(n, 1) / max_exact) / np.log(_MAX_DISTANCE / max_exact)
                         * (nb - max_exact)).astype(np.int32)
    large = np.minimum(large, nb - 1)
    return ret + np.where(n < max_exact, n, large)


def _band_bias(rel_bias, group):
    off = np.arange(-_HALF_KEYS, _HALF_KEYS + 1, dtype=np.int32) * _DILATIONS[group]
    heads = slice(group * _HEADS_PER_GROUP, (group + 1) * _HEADS_PER_GROUP)
    per_key = rel_bias[_rel_bucket(off)][:, heads].T.astype(_F32)
    n_keys = 2 * _HALF_KEYS + 1
    period = _K_WINDOW + _Q_BLOCK
    tables = []
    for shift in (_HALF_KEYS, 0, -_HALF_KEYS):
        lead = _Q_BLOCK - 1 - shift
        v = jnp.pad(per_key, ((0, 0), (lead, period - lead - n_keys)), constant_values=_NEG_INF)
        skew = jnp.tile(v, (1, _Q_BLOCK))[:, :_Q_BLOCK * (period - 1)]
        skew = skew.reshape(_HEADS_PER_GROUP, _Q_BLOCK, period - 1)
        tables.append(skew[:, :, _Q_BLOCK - 1:])
    return jnp.stack(tables)


def _mix_body(x_ref, o1_ref, o4_ref, o16_ref, l1_ref, l4_ref, l16_ref,
              pre_ref, post_ref, wz_ref, wga_ref, wgb_ref, lng_ref, lnb_ref, ws_ref, bs_ref,
              watt_ref, wsgu_ref, wout_ref, y_ref, sgu_ref, stage_ref):
    x = x_ref[...]
    tm = x.shape[0]
    hb = _rms(x, pre_ref[...]).astype(_BF16)

    def token_order(ref, d, slab, lanes):
        if d == 1:
            return ref[0, :, lanes]
        for r in range(d):
            stage_ref[slab, pl.ds(r, tm // d, stride=d), :] = ref[r, :, lanes]
        return stage_ref[slab]

    att_halves = []
    for half in range(_GROUP_W // _LANES):
        lanes = slice(half * _LANES, (half + 1) * _LANES)
        slab = 4 * half
        outs = (token_order(o1_ref, 1, 0, lanes), token_order(o4_ref, 4, slab, lanes),
                token_order(o16_ref, 16, slab + 1, lanes))
        lses = (token_order(l1_ref, 1, 0, lanes), token_order(l4_ref, 4, slab + 2, lanes),
                token_order(l16_ref, 16, slab + 3, lanes))
        top = jnp.maximum(jnp.maximum(lses[0], lses[1]), lses[2])
        wts = [jnp.exp(l - top) for l in lses]
        att_halves.append((wts[0] * outs[0] + wts[1] * outs[1] + wts[2] * outs[2])
                          / (wts[0] + wts[1] + wts[2]))
    att = jnp.concatenate(att_halves, axis=1)
    a = _dot(att.astype(_BF16), watt_ref[...])

    z = _dot(hb, wz_ref[...])
    z = 0.5 * z * (1.0 + lax.erf(z * (2.0 ** -0.5)))
    u, vv = z[:, :_SGU_W], z[:, _SGU_W:]
    mu = jnp.mean(vv, axis=-1, keepdims=True)
    var = jnp.mean(jnp.square(vv - mu), axis=-1, keepdims=True)
    vn = ((vv - mu) * lax.rsqrt(var + _EPS) * lng_ref[...] + lnb_ref[...]).astype(_BF16)
    for c in range(tm // _CHUNK):
        rows = slice(c * _CHUNK, (c + 1) * _CHUNK)
        for g in range(_SGU_GROUPS):
            cols = slice(g * _LANES, (g + 1) * _LANES)
            mixed = _dot(ws_ref[g], vn[rows, cols]) + bs_ref[g]
            sgu_ref[rows, cols] = (u[rows, cols] * mixed).astype(_BF16)
    b = _dot(sgu_ref[...], wsgu_ref[...])

    merged = (jax.nn.sigmoid(_dot(hb, wga_ref[...])) * a
              + jax.nn.sigmoid(_dot(hb, wgb_ref[...])) * b)
    y = _dot(merged.astype(_BF16), wout_ref[...])
    y_ref[...] = x + _rms(y, post_ref[...])


def _mix(x, outs, lses, batch, seq,
         pre_g, post_g, wz, wga, wgb, ln_g, ln_b, ws, bs, watt, wsgu, wout):
    tm = _TOKEN_TILE
    tiles = seq // tm
    row = pl.BlockSpec((tm, _D), lambda b, j: (b * tiles + j, 0))
    grp = [pl.BlockSpec((None, d, tm // d, _GROUP_W), lambda b, j: (b, 0, j, 0)) for d in _DILATIONS]
    consts = (pre_g, post_g, wz, wga, wgb, ln_g, ln_b, ws, bs, watt, wsgu, wout)
    n_stage = 2 * sum(d > 1 for d in _DILATIONS) * (_GROUP_W // _LANES)
    return pl.pallas_call(
        _mix_body,
        grid=(batch, tiles),
        in_specs=[row] + grp + grp + [_const_spec(c.shape) for c in consts],
        out_specs=row,
        out_shape=jax.ShapeDtypeStruct((batch * seq, _D), _F32),
        scratch_shapes=[pltpu.VMEM((tm, _SGU_W), _BF16), pltpu.VMEM((n_stage, tm, _LANES), _F32)],
        compiler_params=_params(2),
        name="mix",
    )(x, *outs, *lses, *consts)


def _trunk(x, w):
    b, s, _ = x.shape
    n = b * s
    x1 = _ffn(x.reshape(n, _D), *w["ffn1"], final=False)
    qkv = _qkv(x1, w["mix_pre_g"], w["w_qkv"], b, s)
    ng = len(_DILATIONS)
    outs, lses = [], []
    for g in range(ng):
        o, lse = _att(qkv[g], qkv[ng + g], qkv[2 * ng + g], w["band_bias"][g])
        outs.append(o)
        lses.append(lse)
    x2 = _mix(x1, outs, lses, b, s, *w["mix"])
    y = _ffn(x2, *w["ffn2"], final=True)
    return y.reshape(b, s, _D)


def _row(v):
    return v.reshape(1, -1).astype(_F32)


def _ffn_weights(pre_g, post_g, fin_g, w_gate, w_up, w_down):
    return (_row(pre_g), _row(post_g), _row(fin_g),
            w_gate.astype(_BF16), w_up.astype(_BF16), w_down.astype(_BF16))


def kernel(x_prompt, x_sample, rel_bias, ffn1_pre_g, ffn1_post_g, ffn1_w_gate, ffn1_w_up, ffn1_w_down, mix_pre_g, w_in, sgu_ln_g, sgu_ln_b, sgu_w_s, sgu_b_s, w_att, w_sgu, w_out, mix_post_g, ffn2_pre_g, ffn2_post_g, ffn2_w_gate, ffn2_w_up, ffn2_w_down, final_g):
    assert ffn1_w_gate.shape[0] == 1, "one layer"
    l = 0
    w_in_b = w_in[l].astype(_BF16)
    z0 = 3 * _QKV_W
    w = {
        "ffn1": _ffn_weights(ffn1_pre_g[l], ffn1_post_g[l], final_g[l],
                             ffn1_w_gate[l], ffn1_w_up[l], ffn1_w_down[l]),
        "ffn2": _ffn_weights(ffn2_pre_g[l], ffn2_post_g[l], final_g[l],
                             ffn2_w_gate[l], ffn2_w_up[l], ffn2_w_down[l]),
        "mix_pre_g": _row(mix_pre_g[l]),
        "w_qkv": w_in_b[:, :z0],
        "band_bias": [_band_bias(rel_bias, g) for g in range(len(_DILATIONS))],
        "mix": (
            _row(mix_pre_g[l]), _row(mix_post_g[l]),
            w_in_b[:, z0:z0 + 2 * _SGU_W], w_in_b[:, z0 + 2 * _SGU_W:z0 + 2 * _SGU_W + _D],
            w_in_b[:, z0 + 2 * _SGU_W + _D:],
            _row(sgu_ln_g[l]), _row(sgu_ln_b[l]),
            sgu_w_s[l].astype(_BF16),
            jnp.broadcast_to(sgu_b_s[l].astype(_F32)[:, :, None], (_SGU_GROUPS, _CHUNK, _LANES)),
            w_att[l].astype(_BF16), w_sgu[l].astype(_BF16), w_out[l].astype(_BF16),
        ),
    }
    return (_trunk(x_prompt, w), _trunk(x_sample, w))
```

```python
import functools

import numpy as np
import jax
import jax.numpy as jnp
from jax import lax
from jax.experimental import pallas as pl
from jax.experimental.pallas import tpu as pltpu

_D = 1024
_DFF = 2816
_HEAD_DIM = 64
_HEADS_PER_GROUP = 4
_DILATIONS = (1, 4, 16)
_HALF_KEYS = 64
_GROUP_W = _HEADS_PER_GROUP * _HEAD_DIM
_QKV_W = len(_DILATIONS) * _GROUP_W
_SGU_W = 512
_SGU_GROUPS = 4
_CHUNK = 128
_N_BUCKETS = 32
_MAX_DISTANCE = 1024
_EPS = 1e-6
_NEG_INF = -1e30
_ATT_SCALE = _HEAD_DIM ** -0.5

_LANES = 128
_Q_BLOCK = 128
_K_WINDOW = _Q_BLOCK + 2 * _HALF_KEYS
_ATT_BLOCKS_PER_STEP = 8
_FF_CHUNK = 256
_N_FF_CHUNKS = _DFF // _FF_CHUNK
_TOKEN_TILE = 1024
_ROW_GROUPS = 2
_MIX_ROW_GROUPS = 1
_VMEM_LIMIT = 56 * 1024 * 1024

_BF16 = jnp.bfloat16
_F32 = jnp.float32


def _rms(x, g):
    return x * lax.rsqrt(jnp.mean(x * x, axis=-1, keepdims=True) + _EPS) * g


def _dot(a, b):
    return jnp.dot(a, b, preferred_element_type=_F32)


def _const_spec(shape):
    return pl.BlockSpec(shape, lambda *_: (0,) * len(shape), pipeline_mode=pl.Buffered(1))


def _params(n_axes):
    return pltpu.CompilerParams(
        dimension_semantics=("arbitrary",) * n_axes, vmem_limit_bytes=_VMEM_LIMIT)


def _ffn_body(x_ref, pre_ref, post_ref, fin_ref, wg_ref, wu_ref, wd_ref, o_ref, xb_ref, acc_ref,
              *, final):
    tm = x_ref.shape[0]
    for s in range(_ROW_GROUPS):
        rows = slice(s * tm // _ROW_GROUPS, (s + 1) * tm // _ROW_GROUPS)
        x = x_ref[rows, :]
        xb_ref[rows, :] = _rms(x, pre_ref[...]).astype(_BF16)
        for c in range(_N_FF_CHUNKS):
            xb = xb_ref[rows, :]
            ff = slice(c * _FF_CHUNK, (c + 1) * _FF_CHUNK)
            gate = _dot(xb, wg_ref[:, ff])
            up = _dot(xb, wu_ref[:, ff])
            act = (gate * jax.nn.sigmoid(gate) * up).astype(_BF16)
            part = _dot(act, wd_ref[ff, :])
            if c == 0:
                acc_ref[rows, :] = part
            else:
                acc_ref[rows, :] += part
        y = x + 0.5 * _rms(acc_ref[rows, :], post_ref[...])
        if final:
            y = _rms(y, fin_ref[...])
        o_ref[rows, :] = y


def _ffn(x, pre_g, post_g, fin_g, wg, wu, wd, *, final):
    n = x.shape[0]
    tm = _TOKEN_TILE
    row = pl.BlockSpec((tm, _D), lambda i: (i, 0))
    return pl.pallas_call(
        functools.partial(_ffn_body, final=final),
        grid=(n // tm,),
        in_specs=[row, _const_spec((1, _D)), _const_spec((1, _D)), _const_spec((1, _D)),
                  _const_spec(wg.shape), _const_spec(wu.shape), _const_spec(wd.shape)],
        out_specs=row,
        out_shape=jax.ShapeDtypeStruct((n, _D), _F32),
        scratch_shapes=[pltpu.VMEM((tm, _D), _BF16), pltpu.VMEM((tm, _D), _F32)],
        compiler_params=_params(1),
        name="ffn_final" if final else "ffn",
    )(x, pre_g, post_g, fin_g, wg, wu, wd)


def _qkv_body(x_ref, g_ref, w_ref, *refs):
    out_refs, stage_ref = refs[:-1], refs[-1]
    gm = x_ref.shape[0] // _ROW_GROUPS
    for s in range(_ROW_GROUPS):
        r0 = s * gm
        hb = _rms(x_ref[r0:r0 + gm, :], g_ref[...]).astype(_BF16)
        slab = 0
        for part in range(3):
            for g, d in enumerate(_DILATIONS):
                col = part * _QKV_W + g * _GROUP_W
                res = _dot(hb, w_ref[:, col:col + _GROUP_W])
                if part == 0:
                    res = res * _ATT_SCALE
                o_ref = out_refs[part * len(_DILATIONS) + g]
                if d == 1:
                    o_ref[0, r0:r0 + gm, :] = res.astype(_BF16)
                    continue
                for half in range(_GROUP_W // _LANES):
                    lanes = slice(half * _LANES, (half + 1) * _LANES)
                    stage_ref[slab, r0:r0 + gm, :] = res[:, lanes]
                    for r in range(d):
                        o_ref[r, r0 // d:(r0 + gm) // d, lanes] = stage_ref[
                            slab, pl.ds(r0 + r, gm // d, stride=d), :].astype(_BF16)
                    slab += 1


def _qkv(x, pre_g, w, batch, seq):
    tm = _TOKEN_TILE
    tiles = seq // tm
    n_stage = 3 * sum(d > 1 for d in _DILATIONS) * (_GROUP_W // _LANES)
    out_specs, out_shape = [], []
    for _ in range(3):
        for d in _DILATIONS:
            out_specs.append(pl.BlockSpec((None, d, tm // d, _GROUP_W), lambda b, j: (b, 0, j, 0)))
            out_shape.append(jax.ShapeDtypeStruct((batch, d, seq // d, _GROUP_W), _BF16))
    return pl.pallas_call(
        _qkv_body,
        grid=(batch, tiles),
        in_specs=[pl.BlockSpec((tm, _D), lambda b, j: (b * tiles + j, 0)), _const_spec((1, _D)),
                  _const_spec(w.shape)],
        out_specs=out_specs,
        out_shape=out_shape,
        scratch_shapes=[pltpu.VMEM((n_stage, tm, _LANES), _F32)],
        compiler_params=_params(2),
        name="qkv",
    )(x, pre_g, w)


def _att_body(q_ref, k_ref, v_ref, bias_ref, o_ref, lse_ref, *, seq, blocks, classes):
    step = pl.program_id(2)
    last = seq // _Q_BLOCK - 1
    low_head = lax.broadcasted_iota(jnp.int32, (_Q_BLOCK, _LANES), 1) < _HEAD_DIM
    for blk in range(blocks):
        i = step * blocks + blk
        start = jnp.clip(i * _Q_BLOCK - _HALF_KEYS, 0, seq - _K_WINDOW)
        start = pl.multiple_of(start, _HALF_KEYS)
        variant = jnp.where(i == 0, 0, jnp.where(i == last, 2, 1))
        rows = slice(blk * _Q_BLOCK, (blk + 1) * _Q_BLOCK)
        for r in range(classes):
            for pair in range(_GROUP_W // _LANES):
                cols = slice(pair * _LANES, (pair + 1) * _LANES)
                q2 = q_ref[r, rows, cols]
                k2 = k_ref[r, pl.ds(start, _K_WINDOW), cols]
                v2 = v_ref[r, pl.ds(start, _K_WINDOW), cols]
                zero = jnp.zeros_like(q2)
                qq = jnp.concatenate(
                    [jnp.where(low_head, q2, zero), jnp.where(low_head, zero, q2)], axis=0)
                s = lax.dot_general(qq, k2, (((1,), (1,)), ((), ())), preferred_element_type=_F32)
                s = s + bias_ref[variant, pair]
                m = jnp.max(s, axis=-1, keepdims=True)
                e = jnp.exp(s - m)
                den = jnp.sum(e, axis=-1, keepdims=True)
                out = _dot(e.astype(_BF16), v2) / den
                lse = jnp.broadcast_to(m + jnp.log(den), out.shape)
                o_ref[r, rows, cols] = jnp.where(low_head, out[:_Q_BLOCK], out[_Q_BLOCK:])
                lse_ref[r, rows, cols] = jnp.where(low_head, lse[:_Q_BLOCK], lse[_Q_BLOCK:])


def _att(q, k, v, bias):
    b, d, seq, _ = q.shape
    nb = seq // _Q_BLOCK
    assert seq % _Q_BLOCK == 0 and seq >= _K_WINDOW
    blocks = min(_ATT_BLOCKS_PER_STEP, nb)
    classes = min(_ATT_BLOCKS_PER_STEP // blocks, d)
    assert nb % blocks == 0 and d % classes == 0
    blk = pl.BlockSpec((None, classes, blocks * _Q_BLOCK, _GROUP_W), lambda bi, r, i: (bi, r, i, 0))
    full = pl.BlockSpec((None, classes, seq, _GROUP_W), lambda bi, r, i: (bi, r, 0, 0))
    shape = jax.ShapeDtypeStruct((b, d, seq, _GROUP_W), _F32)
    return pl.pallas_call(
        functools.partial(_att_body, seq=seq, blocks=blocks, classes=classes),
        grid=(b, d // classes, nb // blocks),
        in_specs=[blk, full, full, _const_spec(bias.shape)],
        out_specs=[blk, blk],
        out_shape=[shape, shape],
        compiler_params=_params(3),
        name=f"att_d{d}",
    )(q, k, v, bias)


def _rel_bucket(rel):
    nb = _N_BUCKETS // 2
    max_exact = nb // 2
    ret = (rel > 0).astype(np.int32) * nb
    n = np.abs(rel)
    large = max_exact + (np.log(np.maximum(n, 1) / max_exact) / np.log(_MAX_DISTANCE / max_exact)
                         * (nb - max_exact)).astype(np.int32)
    large = np.minimum(large, nb - 1)
    return ret + np.where(n < max_exact, n, large)


def _band_bias(rel_bias, group):
    off = np.arange(-_HALF_KEYS, _HALF_KEYS + 1, dtype=np.int32) * _DILATIONS[group]
    heads = slice(group * _HEADS_PER_GROUP, (group + 1) * _HEADS_PER_GROUP)
    per_key = rel_bias[_rel_bucket(off)][:, heads].T.astype(_F32)
    n_keys = 2 * _HALF_KEYS + 1
    period = _K_WINDOW + _Q_BLOCK
    tables = []
    for shift in (_HALF_KEYS, 0, -_HALF_KEYS):
        lead = _Q_BLOCK - 1 - shift
        v = jnp.pad(per_key, ((0, 0), (lead, period - lead - n_keys)), constant_values=_NEG_INF)
        skew = jnp.tile(v, (1, _Q_BLOCK))[:, :_Q_BLOCK * (period - 1)]
        skew = skew.reshape(_HEADS_PER_GROUP, _Q_BLOCK, period - 1)
        tables.append(skew[:, :, _Q_BLOCK - 1:])
    return jnp.stack(tables).reshape(3, _GROUP_W // _LANES, 2 * _Q_BLOCK, _K_WINDOW)


def _mix_body(x_ref, o1_ref, o4_ref, o16_ref, l1_ref, l4_ref, l16_ref,
              pre_ref, post_ref, wz_ref, wga_ref, wgb_ref, lng_ref, lnb_ref, ws_ref, bs_ref,
              watt_ref, wsgu_ref, wout_ref, y_ref, sgu_ref, stage_ref):
    gm = x_ref.shape[0] // _MIX_ROW_GROUPS
    for s in range(_MIX_ROW_GROUPS):
        _mix_rows(s * gm, gm, x_ref, o1_ref, o4_ref, o16_ref, l1_ref, l4_ref, l16_ref,
                  pre_ref, post_ref, wz_ref, wga_ref, wgb_ref, lng_ref, lnb_ref, ws_ref, bs_ref,
                  watt_ref, wsgu_ref, wout_ref, y_ref, sgu_ref, stage_ref)


def _mix_rows(r0, gm, x_ref, o1_ref, o4_ref, o16_ref, l1_ref, l4_ref, l16_ref,
              pre_ref, post_ref, wz_ref, wga_ref, wgb_ref, lng_ref, lnb_ref, ws_ref, bs_ref,
              watt_ref, wsgu_ref, wout_ref, y_ref, sgu_ref, stage_ref):
    x = x_ref[r0:r0 + gm, :]
    hb = _rms(x, pre_ref[...]).astype(_BF16)

    def token_order(ref, d, slab, lanes):
        if d == 1:
            return ref[0, r0:r0 + gm, lanes]
        for r in range(d):
            stage_ref[slab, pl.ds(r0 + r, gm // d, stride=d), :] = ref[
                r, r0 // d:(r0 + gm) // d, lanes]
        return stage_ref[slab, r0:r0 + gm, :]

    att_halves = []
    for half in range(_GROUP_W // _LANES):
        lanes = slice(half * _LANES, (half + 1) * _LANES)
        slab = 4 * half
        outs = (token_order(o1_ref, 1, 0, lanes), token_order(o4_ref, 4, slab, lanes),
                token_order(o16_ref, 16, slab + 1, lanes))
        lses = (token_order(l1_ref, 1, 0, lanes), token_order(l4_ref, 4, slab + 2, lanes),
                token_order(l16_ref, 16, slab + 3, lanes))
        top = jnp.maximum(jnp.maximum(lses[0], lses[1]), lses[2])
        wts = [jnp.exp(l - top) for l in lses]
        att_halves.append((wts[0] * outs[0] + wts[1] * outs[1] + wts[2] * outs[2])
                          / (wts[0] + wts[1] + wts[2]))
    att = jnp.concatenate(att_halves, axis=1)
    a = _dot(att.astype(_BF16), watt_ref[...])

    z = _dot(hb, wz_ref[...])
    z = 0.5 * z * (1.0 + lax.erf(z * (2.0 ** -0.5)))
    u, vv = z[:, :_SGU_W], z[:, _SGU_W:]
    mu = jnp.mean(vv, axis=-1, keepdims=True)
    var = jnp.mean(jnp.square(vv - mu), axis=-1, keepdims=True)
    vn = ((vv - mu) * lax.rsqrt(var + _EPS) * lng_ref[...] + lnb_ref[...]).astype(_BF16)
    for c in range(gm // _CHUNK):
        rows = slice(c * _CHUNK, (c + 1) * _CHUNK)
        for g in range(_SGU_GROUPS):
            cols = slice(g * _LANES, (g + 1) * _LANES)
            mixed = _dot(ws_ref[g], vn[rows, cols]) + bs_ref[g]
            sgu_ref[r0 + c * _CHUNK:r0 + (c + 1) * _CHUNK, cols] = (
                u[rows, cols] * mixed).astype(_BF16)
    b = _dot(sgu_ref[r0:r0 + gm, :], wsgu_ref[...])

    merged = (jax.nn.sigmoid(_dot(hb, wga_ref[...])) * a
              + jax.nn.sigmoid(_dot(hb, wgb_ref[...])) * b)
    y = _dot(merged.astype(_BF16), wout_ref[...])
    y_ref[r0:r0 + gm, :] = x + _rms(y, post_ref[...])


def _mix(x, outs, lses, batch, seq,
         pre_g, post_g, wz, wga, wgb, ln_g, ln_b, ws, bs, watt, wsgu, wout):
    tm = _TOKEN_TILE
    tiles = seq // tm
    row = pl.BlockSpec((tm, _D), lambda b, j: (b * tiles + j, 0))
    grp = [pl.BlockSpec((None, d, tm // d, _GROUP_W), lambda b, j: (b, 0, j, 0)) for d in _DILATIONS]
    consts = (pre_g, post_g, wz, wga, wgb, ln_g, ln_b, ws, bs, watt, wsgu, wout)
    n_stage = 2 * sum(d > 1 for d in _DILATIONS) * (_GROUP_W // _LANES)
    return pl.pallas_call(
        _mix_body,
        grid=(batch, tiles),
        in_specs=[row] + grp + grp + [_const_spec(c.shape) for c in consts],
        out_specs=row,
        out_shape=jax.ShapeDtypeStruct((batch * seq, _D), _F32),
        scratch_shapes=[pltpu.VMEM((tm, _SGU_W), _BF16), pltpu.VMEM((n_stage, tm, _LANES), _F32)],
        compiler_params=_params(2),
        name="mix",
    )(x, *outs, *lses, *consts)


def _trunk(x, w):
    b, s, _ = x.shape
    n = b * s
    x1 = _ffn(x.reshape(n, _D), *w["ffn1"], final=False)
    qkv = _qkv(x1, w["mix_pre_g"], w["w_qkv"], b, s)
    ng = len(_DILATIONS)
    outs, lses = [], []
    for g in range(ng):
        o, lse = _att(qkv[g], qkv[ng + g], qkv[2 * ng + g], w["band_bias"][g])
        outs.append(o)
        lses.append(lse)
    x2 = _mix(x1, outs, lses, b, s, *w["mix"])
    y = _ffn(x2, *w["ffn2"], final=True)
    return y.reshape(b, s, _D)


def _row(v):
    return v.reshape(1, -1).astype(_F32)


def _ffn_weights(pre_g, post_g, fin_g, w_gate, w_up, w_down):
    return (_row(pre_g), _row(post_g), _row(fin_g),
            w_gate.astype(_BF16), w_up.astype(_BF16), w_down.astype(_BF16))


def kernel(x_prompt, x_sample, rel_bias, ffn1_pre_g, ffn1_post_g, ffn1_w_gate, ffn1_w_up, ffn1_w_down, mix_pre_g, w_in, sgu_ln_g, sgu_ln_b, sgu_w_s, sgu_b_s, w_att, w_sgu, w_out, mix_post_g, ffn2_pre_g, ffn2_post_g, ffn2_w_gate, ffn2_w_up, ffn2_w_down, final_g):
    assert ffn1_w_gate.shape[0] == 1, "one layer"
    l = 0
    w_in_b = w_in[l].astype(_BF16)
    z0 = 3 * _QKV_W
    w = {
        "ffn1": _ffn_weights(ffn1_pre_g[l], ffn1_post_g[l], final_g[l],
                             ffn1_w_gate[l], ffn1_w_up[l], ffn1_w_down[l]),
        "ffn2": _ffn_weights(ffn2_pre_g[l], ffn2_post_g[l], final_g[l],
                             ffn2_w_gate[l], ffn2_w_up[l], ffn2_w_down[l]),
        "mix_pre_g": _row(mix_pre_g[l]),
        "w_qkv": w_in_b[:, :z0],
        "band_bias": [_band_bias(rel_bias, g) for g in range(len(_DILATIONS))],
        "mix": (
            _row(mix_pre_g[l]), _row(mix_post_g[l]),
            w_in_b[:, z0:z0 + 2 * _SGU_W], w_in_b[:, z0 + 2 * _SGU_W:z0 + 2 * _SGU_W + _D],
            w_in_b[:, z0 + 2 * _SGU_W + _D:],
            _row(sgu_ln_g[l]), _row(sgu_ln_b[l]),
            sgu_w_s[l].astype(_BF16),
            jnp.broadcast_to(sgu_b_s[l].astype(_F32)[:, :, None], (_SGU_GROUPS, _CHUNK, _LANES)),
            w_att[l].astype(_BF16), w_sgu[l].astype(_BF16), w_out[l].astype(_BF16),
        ),
    }
    return (_trunk(x_prompt, w), _trunk(x_sample, w))
```

```python
import functools

import numpy as np
import jax
import jax.numpy as jnp
from jax import lax
from jax.experimental import pallas as pl
from jax.experimental.pallas import tpu as pltpu

_D = 1024
_DFF = 2816
_HEAD_DIM = 64
_HEADS_PER_GROUP = 4
_DILATIONS = (1, 4, 16)
_HALF_KEYS = 64
_GROUP_W = _HEADS_PER_GROUP * _HEAD_DIM
_QKV_W = len(_DILATIONS) * _GROUP_W
_SGU_W = 512
_SGU_GROUPS = 4
_CHUNK = 128
_N_BUCKETS = 32
_MAX_DISTANCE = 1024
_EPS = 1e-6
_NEG_INF = -1e30
_ATT_SCALE = _HEAD_DIM ** -0.5

_LANES = 128
_Q_BLOCK = 128
_K_WINDOW = _Q_BLOCK + 2 * _HALF_KEYS
_ATT_BLOCKS_PER_STEP = 16
_FF_CHUNK = 256
_N_FF_CHUNKS = _DFF // _FF_CHUNK
_TOKEN_TILE = 1024
_ROW_GROUPS = 2
_MIX_ROW_GROUPS = 1
_VMEM_LIMIT = 56 * 1024 * 1024

_BF16 = jnp.bfloat16
_F32 = jnp.float32


def _rms(x, g):
    return x * lax.rsqrt(jnp.mean(x * x, axis=-1, keepdims=True) + _EPS) * g


def _dot(a, b):
    return jnp.dot(a, b, preferred_element_type=_F32)


def _const_spec(shape):
    return pl.BlockSpec(shape, lambda *_: (0,) * len(shape), pipeline_mode=pl.Buffered(1))


def _params(n_axes):
    return pltpu.CompilerParams(
        dimension_semantics=("arbitrary",) * n_axes, vmem_limit_bytes=_VMEM_LIMIT)


def _ffn_body(x_ref, pre_ref, post_ref, fin_ref, wg_ref, wu_ref, wd_ref, o_ref, xb_ref, acc_ref,
              *, final):
    tm = x_ref.shape[0]
    for s in range(_ROW_GROUPS):
        rows = slice(s * tm // _ROW_GROUPS, (s + 1) * tm // _ROW_GROUPS)
        x = x_ref[rows, :]
        xb_ref[rows, :] = _rms(x, pre_ref[...]).astype(_BF16)
        for c in range(_N_FF_CHUNKS):
            xb = xb_ref[rows, :]
            ff = slice(c * _FF_CHUNK, (c + 1) * _FF_CHUNK)
            gate = _dot(xb, wg_ref[:, ff])
            up = _dot(xb, wu_ref[:, ff])
            act = (gate * jax.nn.sigmoid(gate) * up).astype(_BF16)
            part = _dot(act, wd_ref[ff, :])
            if c == 0:
                acc_ref[rows, :] = part
            else:
                acc_ref[rows, :] += part
        y = x + 0.5 * _rms(acc_ref[rows, :], post_ref[...])
        if final:
            y = _rms(y, fin_ref[...])
        o_ref[rows, :] = y


def _ffn(x, pre_g, post_g, fin_g, wg, wu, wd, *, final):
    n = x.shape[0]
    tm = _TOKEN_TILE
    row = pl.BlockSpec((tm, _D), lambda i: (i, 0))
    return pl.pallas_call(
        functools.partial(_ffn_body, final=final),
        grid=(n // tm,),
        in_specs=[row, _const_spec((1, _D)), _const_spec((1, _D)), _const_spec((1, _D)),
                  _const_spec(wg.shape), _const_spec(wu.shape), _const_spec(wd.shape)],
        out_specs=row,
        out_shape=jax.ShapeDtypeStruct((n, _D), _F32),
        scratch_shapes=[pltpu.VMEM((tm, _D), _BF16), pltpu.VMEM((tm, _D), _F32)],
        compiler_params=_params(1),
        name="ffn_final" if final else "ffn",
    )(x, pre_g, post_g, fin_g, wg, wu, wd)


def _qkv_body(x_ref, g_ref, w_ref, *refs):
    out_refs, stage_ref = refs[:-1], refs[-1]
    gm = x_ref.shape[0] // _ROW_GROUPS
    for s in range(_ROW_GROUPS):
        r0 = s * gm
        hb = _rms(x_ref[r0:r0 + gm, :], g_ref[...]).astype(_BF16)
        slab = 0
        for part in range(3):
            for g, d in enumerate(_DILATIONS):
                col = part * _QKV_W + g * _GROUP_W
                res = _dot(hb, w_ref[:, col:col + _GROUP_W])
                if part == 0:
                    res = res * _ATT_SCALE
                o_ref = out_refs[part * len(_DILATIONS) + g]
                if d == 1:
                    o_ref[0, r0:r0 + gm, :] = res.astype(_BF16)
                    continue
                for half in range(_GROUP_W // _LANES):
                    lanes = slice(half * _LANES, (half + 1) * _LANES)
                    stage_ref[slab, r0:r0 + gm, :] = res[:, lanes]
                    for r in range(d):
                        o_ref[r, r0 // d:(r0 + gm) // d, lanes] = stage_ref[
                            slab, pl.ds(r0 + r, gm // d, stride=d), :].astype(_BF16)
                    slab += 1


def _qkv(x, pre_g, w, batch, seq):
    tm = _TOKEN_TILE
    tiles = seq // tm
    n_stage = 3 * sum(d > 1 for d in _DILATIONS) * (_GROUP_W // _LANES)
    out_specs, out_shape = [], []
    for _ in range(3):
        for d in _DILATIONS:
            out_specs.append(pl.BlockSpec((None, d, tm // d, _GROUP_W), lambda b, j: (b, 0, j, 0)))
            out_shape.append(jax.ShapeDtypeStruct((batch, d, seq // d, _GROUP_W), _BF16))
    return pl.pallas_call(
        _qkv_body,
        grid=(batch, tiles),
        in_specs=[pl.BlockSpec((tm, _D), lambda b, j: (b * tiles + j, 0)), _const_spec((1, _D)),
                  _const_spec(w.shape)],
        out_specs=out_specs,
        out_shape=out_shape,
        scratch_shapes=[pltpu.VMEM((n_stage, tm, _LANES), _F32)],
        compiler_params=_params(2),
        name="qkv",
    )(x, pre_g, w)


def _att_body(q_ref, k_ref, v_ref, bias_ref, o_ref, lse_ref, *, seq, blocks, classes):
    step = pl.program_id(2)
    last = seq // _Q_BLOCK - 1
    low_head = lax.broadcasted_iota(jnp.int32, (_Q_BLOCK, _LANES), 1) < _HEAD_DIM
    for blk in range(blocks):
        i = step * blocks + blk
        start = jnp.clip(i * _Q_BLOCK - _HALF_KEYS, 0, seq - _K_WINDOW)
        start = pl.multiple_of(start, _HALF_KEYS)
        variant = jnp.where(i == 0, 0, jnp.where(i == last, 2, 1))
        rows = slice(blk * _Q_BLOCK, (blk + 1) * _Q_BLOCK)
        for r in range(classes):
            for pair in range(_GROUP_W // _LANES):
                cols = slice(pair * _LANES, (pair + 1) * _LANES)
                q2 = q_ref[r, rows, cols]
                k2 = k_ref[r, pl.ds(start, _K_WINDOW), cols]
                v2 = v_ref[r, pl.ds(start, _K_WINDOW), cols]
                zero = jnp.zeros_like(q2)
                qq = jnp.concatenate(
                    [jnp.where(low_head, q2, zero), jnp.where(low_head, zero, q2)], axis=0)
                s = lax.dot_general(qq, k2, (((1,), (1,)), ((), ())), preferred_element_type=_F32)
                s = s + bias_ref[variant, pair]
                m = jnp.max(s, axis=-1, keepdims=True)
                e = jnp.exp(s - m)
                den = jnp.sum(e, axis=-1, keepdims=True)
                out = _dot(e.astype(_BF16), v2) / den
                lse = jnp.broadcast_to(m + jnp.log(den), out.shape)
                o_ref[r, rows, cols] = jnp.where(low_head, out[:_Q_BLOCK], out[_Q_BLOCK:])
                lse_ref[r, rows, cols] = jnp.where(low_head, lse[:_Q_BLOCK], lse[_Q_BLOCK:])


def _att(q, k, v, bias):
    b, d, seq, _ = q.shape
    nb = seq // _Q_BLOCK
    assert seq % _Q_BLOCK == 0 and seq >= _K_WINDOW
    blocks = min(_ATT_BLOCKS_PER_STEP, nb)
    classes = min(_ATT_BLOCKS_PER_STEP // blocks, d)
    assert nb % blocks == 0 and d % classes == 0
    blk = pl.BlockSpec((None, classes, blocks * _Q_BLOCK, _GROUP_W), lambda bi, r, i: (bi, r, i, 0))
    full = pl.BlockSpec((None, classes, seq, _GROUP_W), lambda bi, r, i: (bi, r, 0, 0))
    shape = jax.ShapeDtypeStruct((b, d, seq, _GROUP_W), _F32)
    return pl.pallas_call(
        functools.partial(_att_body, seq=seq, blocks=blocks, classes=classes),
        grid=(b, d // classes, nb // blocks),
        in_specs=[blk, full, full, _const_spec(bias.shape)],
        out_specs=[blk, blk],
        out_shape=[shape, shape],
        compiler_params=_params(3),
        name=f"att_d{d}",
    )(q, k, v, bias)


def _rel_bucket(rel):
    nb = _N_BUCKETS // 2
    max_exact = nb // 2
    ret = (rel > 0).astype(np.int32) * nb
    n = np.abs(rel)
    large = max_exact + (np.log(np.maximum(n, 1) / max_exact) / np.log(_MAX_DISTANCE / max_exact)
                         * (nb - max_exact)).astype(np.int32)
    large = np.minimum(large, nb - 1)
    return ret + np.where(n < max_exact, n, large)


def _band_bias(rel_bias, group):
    off = np.arange(-_HALF_KEYS, _HALF_KEYS + 1, dtype=np.int32) * _DILATIONS[group]
    heads = slice(group * _HEADS_PER_GROUP, (group + 1) * _HEADS_PER_GROUP)
    per_key = rel_bias[_rel_bucket(off)][:, heads].T.astype(_F32)
    n_keys = 2 * _HALF_KEYS + 1
    period = _K_WINDOW + _Q_BLOCK
    tables = []
    for shift in (_HALF_KEYS, 0, -_HALF_KEYS):
        lead = _Q_BLOCK - 1 - shift
        v = jnp.pad(per_key, ((0, 0), (lead, period - lead - n_keys)), constant_values=_NEG_INF)
        skew = jnp.tile(v, (1, _Q_BLOCK))[:, :_Q_BLOCK * (period - 1)]
        skew = skew.reshape(_HEADS_PER_GROUP, _Q_BLOCK, period - 1)
        tables.append(skew[:, :, _Q_BLOCK - 1:])
    return jnp.stack(tables).reshape(3, _GROUP_W // _LANES, 2 * _Q_BLOCK, _K_WINDOW)


def _mix_body(x_ref, o1_ref, o4_ref, o16_ref, l1_ref, l4_ref, l16_ref,
              pre_ref, post_ref, wz_ref, wga_ref, wgb_ref, lng_ref, lnb_ref, ws_ref, bs_ref,
              watt_ref, wsgu_ref, wout_ref, y_ref, sgu_ref, stage_ref):
    gm = x_ref.shape[0] // _MIX_ROW_GROUPS
    for s in range(_MIX_ROW_GROUPS):
        _mix_rows(s * gm, gm, x_ref, o1_ref, o4_ref, o16_ref, l1_ref, l4_ref, l16_ref,
                  pre_ref, post_ref, wz_ref, wga_ref, wgb_ref, lng_ref, lnb_ref, ws_ref, bs_ref,
                  watt_ref, wsgu_ref, wout_ref, y_ref, sgu_ref, stage_ref)


def _mix_rows(r0, gm, x_ref, o1_ref, o4_ref, o16_ref, l1_ref, l4_ref, l16_ref,
              pre_ref, post_ref, wz_ref, wga_ref, wgb_ref, lng_ref, lnb_ref, ws_ref, bs_ref,
              watt_ref, wsgu_ref, wout_ref, y_ref, sgu_ref, stage_ref):
    x = x_ref[r0:r0 + gm, :]
    hb = _rms(x, pre_ref[...]).astype(_BF16)

    z = _dot(hb, wz_ref[...])
    gate_a = jax.nn.sigmoid(_dot(hb, wga_ref[...]))

    z = 0.5 * z * (1.0 + lax.erf(z * (2.0 ** -0.5)))
    u, vv = z[:, :_SGU_W], z[:, _SGU_W:]
    mu = jnp.mean(vv, axis=-1, keepdims=True)
    var = jnp.mean(jnp.square(vv - mu), axis=-1, keepdims=True)
    vn = ((vv - mu) * lax.rsqrt(var + _EPS) * lng_ref[...] + lnb_ref[...]).astype(_BF16)

    def token_order(ref, d, slab, lanes):
        if d == 1:
            return ref[0, r0:r0 + gm, lanes]
        for r in range(d):
            stage_ref[slab, pl.ds(r0 + r, gm // d, stride=d), :] = ref[
                r, r0 // d:(r0 + gm) // d, lanes]
        return stage_ref[slab, r0:r0 + gm, :]

    att_halves = []
    for half in range(_GROUP_W // _LANES):
        lanes = slice(half * _LANES, (half + 1) * _LANES)
        slab = 4 * half
        outs = (token_order(o1_ref, 1, 0, lanes), token_order(o4_ref, 4, slab, lanes),
                token_order(o16_ref, 16, slab + 1, lanes))
        lses = (token_order(l1_ref, 1, 0, lanes), token_order(l4_ref, 4, slab + 2, lanes),
                token_order(l16_ref, 16, slab + 3, lanes))
        top = jnp.maximum(jnp.maximum(lses[0], lses[1]), lses[2])
        wts = [jnp.exp(l - top) for l in lses]
        att_halves.append((wts[0] * outs[0] + wts[1] * outs[1] + wts[2] * outs[2])
                          / (wts[0] + wts[1] + wts[2]))
    att = jnp.concatenate(att_halves, axis=1)
    gate_b = jax.nn.sigmoid(_dot(hb, wgb_ref[...]))
    a = _dot(att.astype(_BF16), watt_ref[...])

    for c in range(gm // _CHUNK):
        rows = slice(c * _CHUNK, (c + 1) * _CHUNK)
        for g in range(_SGU_GROUPS):
            cols = slice(g * _LANES, (g + 1) * _LANES)
            mixed = _dot(ws_ref[g], vn[rows, cols]) + bs_ref[g]
            sgu_ref[r0 + c * _CHUNK:r0 + (c + 1) * _CHUNK, cols] = (
                u[rows, cols] * mixed).astype(_BF16)
    b = _dot(sgu_ref[r0:r0 + gm, :], wsgu_ref[...])

    merged = gate_a * a + gate_b * b
    y = _dot(merged.astype(_BF16), wout_ref[...])
    y_ref[r0:r0 + gm, :] = x + _rms(y, post_ref[...])


def _mix(x, outs, lses, batch, seq,
         pre_g, post_g, wz, wga, wgb, ln_g, ln_b, ws, bs, watt, wsgu, wout):
    tm = _TOKEN_TILE
    tiles = seq // tm
    row = pl.BlockSpec((tm, _D), lambda b, j: (b * tiles + j, 0))
    grp = [pl.BlockSpec((None, d, tm // d, _GROUP_W), lambda b, j: (b, 0, j, 0)) for d in _DILATIONS]
    consts = (pre_g, post_g, wz, wga, wgb, ln_g, ln_b, ws, bs, watt, wsgu, wout)
    n_stage = 2 * sum(d > 1 for d in _DILATIONS) * (_GROUP_W // _LANES)
    return pl.pallas_call(
        _mix_body,
        grid=(batch, tiles),
        in_specs=[row] + grp + grp + [_const_spec(c.shape) for c in consts],
        out_specs=row,
        out_shape=jax.ShapeDtypeStruct((batch * seq, _D), _F32),
        scratch_shapes=[pltpu.VMEM((tm, _SGU_W), _BF16), pltpu.VMEM((n_stage, tm, _LANES), _F32)],
        compiler_params=_params(2),
        name="mix",
    )(x, *outs, *lses, *consts)


def _trunk(x, w):
    b, s, _ = x.shape
    n = b * s
    x1 = _ffn(x.reshape(n, _D), *w["ffn1"], final=False)
    qkv = _qkv(x1, w["mix_pre_g"], w["w_qkv"], b, s)
    ng = len(_DILATIONS)
    outs, lses = [], []
    for g in range(ng):
        o, lse = _att(qkv[g], qkv[ng + g], qkv[2 * ng + g], w["band_bias"][g])
        outs.append(o)
        lses.append(lse)
    x2 = _mix(x1, outs, lses, b, s, *w["mix"])
    y = _ffn(x2, *w["ffn2"], final=True)
    return y.reshape(b, s, _D)


def _row(v):
    return v.reshape(1, -1).astype(_F32)


def _ffn_weights(pre_g, post_g, fin_g, w_gate, w_up, w_down):
    return (_row(pre_g), _row(post_g), _row(fin_g),
            w_gate.astype(_BF16), w_up.astype(_BF16), w_down.astype(_BF16))


def kernel(x_prompt, x_sample, rel_bias, ffn1_pre_g, ffn1_post_g, ffn1_w_gate, ffn1_w_up, ffn1_w_down, mix_pre_g, w_in, sgu_ln_g, sgu_ln_b, sgu_w_s, sgu_b_s, w_att, w_sgu, w_out, mix_post_g, ffn2_pre_g, ffn2_post_g, ffn2_w_gate, ffn2_w_up, ffn2_w_down, final_g):
    assert ffn1_w_gate.shape[0] == 1, "one layer"
    l = 0
    w_in_b = w_in[l].astype(_BF16)
    z0 = 3 * _QKV_W
    w = {
        "ffn1": _ffn_weights(ffn1_pre_g[l], ffn1_post_g[l], final_g[l],
                             ffn1_w_gate[l], ffn1_w_up[l], ffn1_w_down[l]),
        "ffn2": _ffn_weights(ffn2_pre_g[l], ffn2_post_g[l], final_g[l],
                             ffn2_w_gate[l], ffn2_w_up[l], ffn2_w_down[l]),
        "mix_pre_g": _row(mix_pre_g[l]),
        "w_qkv": w_in_b[:, :z0],
        "band_bias": [_band_bias(rel_bias, g) for g in range(len(_DILATIONS))],
        "mix": (
            _row(mix_pre_g[l]), _row(mix_post_g[l]),
            w_in_b[:, z0:z0 + 2 * _SGU_W], w_in_b[:, z0 + 2 * _SGU_W:z0 + 2 * _SGU_W + _D],
            w_in_b[:, z0 + 2 * _SGU_W + _D:],
            _row(sgu_ln_g[l]), _row(sgu_ln_b[l]),
            sgu_w_s[l].astype(_BF16),
            jnp.broadcast_to(sgu_b_s[l].astype(_F32)[:, :, None], (_SGU_GROUPS, _CHUNK, _LANES)),
            w_att[l].astype(_BF16), w_sgu[l].astype(_BF16), w_out[l].astype(_BF16),
        ),
    }
    return (_trunk(x_prompt, w), _trunk(x_sample, w))
```

```python
import functools
import math

import numpy as np
import jax
import jax.numpy as jnp
from jax import lax
from jax.experimental import pallas as pl
from jax.experimental.pallas import tpu as pltpu

_D = 1024
_DFF = 2816
_HEAD_DIM = 64
_HEADS_PER_GROUP = 4
_DILATIONS = (1, 4, 16)
_HALF_KEYS = 64
_GROUP_W = _HEADS_PER_GROUP * _HEAD_DIM
_QKV_W = len(_DILATIONS) * _GROUP_W
_SGU_W = 512
_SGU_GROUPS = 4
_CHUNK = 128
_N_BUCKETS = 32
_MAX_DISTANCE = 1024
_EPS = 1e-6
_NEG_INF = -1e30
_ATT_SCALE = _HEAD_DIM ** -0.5

_LANES = 128
_Q_BLOCK = 128
_K_WINDOW = _Q_BLOCK + 2 * _HALF_KEYS
_ATT_UNITS_PER_STEP = 16
_FF_CHUNK = 256
_N_FF_CHUNKS = _DFF // _FF_CHUNK
_TOKEN_TILE = 1024
_ROW_GROUPS = 2
_VMEM_LIMIT = 56 * 1024 * 1024

_BF16 = jnp.bfloat16
_F32 = jnp.float32


def _rms(x, g):
    return x * lax.rsqrt(jnp.mean(x * x, axis=-1, keepdims=True) + _EPS) * g


def _dot(a, b):
    return jnp.dot(a, b, preferred_element_type=_F32)


def _const_spec(shape):
    return pl.BlockSpec(shape, lambda *_: (0,) * len(shape), pipeline_mode=pl.Buffered(1))


def _params(n_axes):
    return pltpu.CompilerParams(
        dimension_semantics=("arbitrary",) * n_axes, vmem_limit_bytes=_VMEM_LIMIT)


def _ffn_body(*refs, n_in, n_out, final):
    x_refs = refs[:n_in]
    pre_ref, post_ref, fin_ref, wg_ref, wu_ref, wd_ref = refs[n_in:n_in + 6]
    o_refs = refs[n_in + 6:n_in + 6 + n_out]
    xb_ref, acc_ref = refs[n_in + 6 + n_out:]
    step = pl.program_id(0)
    steps = pl.num_programs(0)
    tm = xb_ref.shape[0]
    for s in range(_ROW_GROUPS):
        rows = slice(s * tm // _ROW_GROUPS, (s + 1) * tm // _ROW_GROUPS)
        x = x_refs[0][rows, :]
        for k in range(1, n_in):
            x = jnp.where(step * n_in >= k * steps, x_refs[k][rows, :], x)
        xb_ref[rows, :] = _rms(x, pre_ref[...]).astype(_BF16)
        for c in range(_N_FF_CHUNKS):
            xb = xb_ref[rows, :]
            ff = slice(c * _FF_CHUNK, (c + 1) * _FF_CHUNK)
            gate = _dot(xb, wg_ref[:, ff])
            up = _dot(xb, wu_ref[:, ff])
            act = (gate * jax.nn.sigmoid(gate) * up).astype(_BF16)
            part = _dot(act, wd_ref[ff, :])
            if c == 0:
                acc_ref[rows, :] = part
            else:
                acc_ref[rows, :] += part
        y = x + 0.5 * _rms(acc_ref[rows, :], post_ref[...])
        if final:
            y = _rms(y, fin_ref[...])
        o_refs[-1][rows, :] = y
    for k in range(n_out - 1):
        @pl.when((step * n_out >= k * steps) & (step * n_out < (k + 1) * steps))
        def _():
            o_refs[k][...] = o_refs[-1][...]


def _ffn(xs, pre_g, post_g, fin_g, wg, wu, wd, *, n_out, final):
    n_in = len(xs)
    tm = _TOKEN_TILE
    tiles_in = xs[0].shape[0] // tm
    steps = n_in * tiles_in
    assert all(x.shape == xs[0].shape for x in xs) and steps % n_out == 0
    tiles_out = steps // n_out

    def in_map(k):
        return lambda i: (jnp.clip(i - k * tiles_in, 0, tiles_in - 1), 0)

    def out_map(k):
        return lambda i: (jnp.clip(i - k * tiles_out, 0, tiles_out - 1), 0)

    out_shape = [jax.ShapeDtypeStruct((tiles_out * tm, _D), _F32)] * n_out
    return pl.pallas_call(
        functools.partial(_ffn_body, n_in=n_in, n_out=n_out, final=final),
        grid=(steps,),
        in_specs=[pl.BlockSpec((tm, _D), in_map(k)) for k in range(n_in)]
        + [_const_spec((1, _D)), _const_spec((1, _D)), _const_spec((1, _D)),
           _const_spec(wg.shape), _const_spec(wu.shape), _const_spec(wd.shape)],
        out_specs=[pl.BlockSpec((tm, _D), out_map(k)) for k in range(n_out)],
        out_shape=out_shape,
        scratch_shapes=[pltpu.VMEM((tm, _D), _BF16), pltpu.VMEM((tm, _D), _F32)],
        compiler_params=_params(1),
        name="ffn_final" if final else "ffn",
    )(*xs, pre_g, post_g, fin_g, wg, wu, wd)


def _qkv_body(x_ref, g_ref, w_ref, *refs):
    out_refs, stage_ref = refs[:-1], refs[-1]
    gm = x_ref.shape[0] // _ROW_GROUPS
    for s in range(_ROW_GROUPS):
        r0 = s * gm
        hb = _rms(x_ref[r0:r0 + gm, :], g_ref[...]).astype(_BF16)
        slab = 0
        for part in range(3):
            for g, d in enumerate(_DILATIONS):
                col = part * _QKV_W + g * _GROUP_W
                res = _dot(hb, w_ref[:, col:col + _GROUP_W])
                if part == 0:
                    res = res * _ATT_SCALE
                o_ref = out_refs[part * len(_DILATIONS) + g]
                if d == 1:
                    o_ref[0, r0:r0 + gm, :] = res.astype(_BF16)
                    continue
                for half in range(_GROUP_W // _LANES):
                    lanes = slice(half * _LANES, (half + 1) * _LANES)
                    stage_ref[slab, r0:r0 + gm, :] = res[:, lanes]
                    for r in range(d):
                        o_ref[r, r0 // d:(r0 + gm) // d, lanes] = stage_ref[
                            slab, pl.ds(r0 + r, gm // d, stride=d), :].astype(_BF16)
                    slab += 1


def _qkv(x, pre_g, w, segment):
    tm = _TOKEN_TILE
    tiles = segment // tm
    n_seg = x.shape[0] // segment
    n_stage = 3 * sum(d > 1 for d in _DILATIONS) * (_GROUP_W // _LANES)
    out_specs, out_shape = [], []
    for _ in range(3):
        for d in _DILATIONS:
            out_specs.append(pl.BlockSpec((None, d, tm // d, _GROUP_W), lambda v, j: (v, 0, j, 0)))
            out_shape.append(jax.ShapeDtypeStruct((n_seg, d, segment // d, _GROUP_W), _BF16))
    return pl.pallas_call(
        _qkv_body,
        grid=(n_seg, tiles),
        in_specs=[pl.BlockSpec((tm, _D), lambda v, j: (v * tiles + j, 0)), _const_spec((1, _D)),
                  _const_spec(w.shape)],
        out_specs=out_specs,
        out_shape=out_shape,
        scratch_shapes=[pltpu.VMEM((n_stage, tm, _LANES), _F32)],
        compiler_params=_params(2),
        name="qkv",
    )(x, pre_g, w)


def _att_body(q_ref, k_ref, v_ref, kp_ref, kn_ref, vp_ref, vn_ref, bias_ref, o_ref, lse_ref,
              *, seq, blocks, classes, layout):
    seg = pl.program_id(0)
    step = pl.program_id(2)
    last = seq // _Q_BLOCK - 1
    pos, per, base = seg, layout[0][1], 0
    for n_seg, seg_per_seq in layout:
        inside = (seg >= base) & (seg < base + n_seg)
        pos = jnp.where(inside, (seg - base) % seg_per_seq, pos)
        per = jnp.where(inside, seg_per_seq, per)
        base += n_seg
    has_prev = pos != 0
    has_next = pos != per - 1
    tail = _K_WINDOW - _HALF_KEYS
    low_head = lax.broadcasted_iota(jnp.int32, (_Q_BLOCK, _LANES), 1) < _HEAD_DIM
    for blk in range(blocks):
        i = step * blocks + blk
        start = jnp.clip(i * _Q_BLOCK - _HALF_KEYS, 0, seq - _K_WINDOW)
        start = pl.multiple_of(start, _HALF_KEYS)
        halo_lo = (i == 0) & has_prev
        halo_hi = (i == last) & has_next
        variant = jnp.where((i == 0) & jnp.logical_not(has_prev), 0,
                            jnp.where((i == last) & jnp.logical_not(has_next), 2, 1))
        rows = slice(blk * _Q_BLOCK, (blk + 1) * _Q_BLOCK)
        for r in range(classes):
            for pair in range(_GROUP_W // _LANES):
                cols = slice(pair * _LANES, (pair + 1) * _LANES)
                q2 = q_ref[r, rows, cols]
                k2 = k_ref[r, pl.ds(start, _K_WINDOW), cols]
                v2 = v_ref[r, pl.ds(start, _K_WINDOW), cols]
                if blk == 0:
                    k2 = jnp.where(halo_lo, jnp.concatenate(
                        [kp_ref[r, :, cols], k_ref[r, 0:tail, cols]], axis=0), k2)
                    v2 = jnp.where(halo_lo, jnp.concatenate(
                        [vp_ref[r, :, cols], v_ref[r, 0:tail, cols]], axis=0), v2)
                if blk == blocks - 1:
                    k2 = jnp.where(halo_hi, jnp.concatenate(
                        [k_ref[r, seq - tail:seq, cols], kn_ref[r, :, cols]], axis=0), k2)
                    v2 = jnp.where(halo_hi, jnp.concatenate(
                        [v_ref[r, seq - tail:seq, cols], vn_ref[r, :, cols]], axis=0), v2)
                zero = jnp.zeros_like(q2)
                qq = jnp.concatenate(
                    [jnp.where(low_head, q2, zero), jnp.where(low_head, zero, q2)], axis=0)
                s = lax.dot_general(qq, k2, (((1,), (1,)), ((), ())), preferred_element_type=_F32)
                s = s + bias_ref[variant, pair]
                m = jnp.max(s, axis=-1, keepdims=True)
                e = jnp.exp(s - m)
                den = jnp.sum(e, axis=-1, keepdims=True)
                out = _dot(e.astype(_BF16), v2) / den
                lse = jnp.broadcast_to(m + jnp.log(den), out.shape)
                o_ref[r, rows, cols] = jnp.where(low_head, out[:_Q_BLOCK], out[_Q_BLOCK:])
                lse_ref[r, rows, cols] = jnp.where(low_head, lse[:_Q_BLOCK], lse[_Q_BLOCK:])


def _att(q, k, v, bias, layout):
    n_seg, d, seq, _ = q.shape
    nb = seq // _Q_BLOCK
    assert seq % _Q_BLOCK == 0 and seq >= _K_WINDOW and sum(n for n, _ in layout) == n_seg
    blocks = min(_ATT_UNITS_PER_STEP, nb)
    classes = min(_ATT_UNITS_PER_STEP // blocks, d)
    assert nb % blocks == 0 and d % classes == 0 and blocks >= 2
    halo_blocks = seq // _HALF_KEYS
    blk = pl.BlockSpec((None, classes, blocks * _Q_BLOCK, _GROUP_W), lambda s, r, i: (s, r, i, 0))
    full = pl.BlockSpec((None, classes, seq, _GROUP_W), lambda s, r, i: (s, r, 0, 0))
    prev = pl.BlockSpec((None, classes, _HALF_KEYS, _GROUP_W),
                        lambda s, r, i: (jnp.maximum(s - 1, 0), r, halo_blocks - 1, 0))
    nxt = pl.BlockSpec((None, classes, _HALF_KEYS, _GROUP_W),
                       lambda s, r, i: (jnp.minimum(s + 1, n_seg - 1), r, 0, 0))
    shape = jax.ShapeDtypeStruct((n_seg, d, seq, _GROUP_W), _F32)
    return pl.pallas_call(
        functools.partial(_att_body, seq=seq, blocks=blocks, classes=classes, layout=layout),
        grid=(n_seg, d // classes, nb // blocks),
        in_specs=[blk, full, full, prev, nxt, prev, nxt, _const_spec(bias.shape)],
        out_specs=[blk, blk],
        out_shape=[shape, shape],
        compiler_params=_params(3),
        name=f"att_d{d}",
    )(q, k, v, k, k, v, v, bias)


def _rel_bucket(rel):
    nb = _N_BUCKETS // 2
    max_exact = nb // 2
    ret = (rel > 0).astype(np.int32) * nb
    n = np.abs(rel)
    large = max_exact + (np.log(np.maximum(n, 1) / max_exact) / np.log(_MAX_DISTANCE / max_exact)
                         * (nb - max_exact)).astype(np.int32)
    large = np.minimum(large, nb - 1)
    return ret + np.where(n < max_exact, n, large)


def _band_bias(rel_bias, group):
    off = np.arange(-_HALF_KEYS, _HALF_KEYS + 1, dtype=np.int32) * _DILATIONS[group]
    heads = slice(group * _HEADS_PER_GROUP, (group + 1) * _HEADS_PER_GROUP)
    per_key = rel_bias[_rel_bucket(off)][:, heads].T.astype(_F32)
    n_keys = 2 * _HALF_KEYS + 1
    period = _K_WINDOW + _Q_BLOCK
    tables = []
    for shift in (_HALF_KEYS, 0, -_HALF_KEYS):
        lead = _Q_BLOCK - 1 - shift
        v = jnp.pad(per_key, ((0, 0), (lead, period - lead - n_keys)), constant_values=_NEG_INF)
        skew = jnp.tile(v, (1, _Q_BLOCK))[:, :_Q_BLOCK * (period - 1)]
        skew = skew.reshape(_HEADS_PER_GROUP, _Q_BLOCK, period - 1)
        tables.append(skew[:, :, _Q_BLOCK - 1:])
    return jnp.stack(tables).reshape(3, _GROUP_W // _LANES, 2 * _Q_BLOCK, _K_WINDOW)


def _mix_body(x_ref, o1_ref, o4_ref, o16_ref, l1_ref, l4_ref, l16_ref,
              pre_ref, post_ref, wz_ref, wga_ref, wgb_ref, lng_ref, lnb_ref, ws_ref, bs_ref,
              watt_ref, wsgu_ref, wout_ref, y_ref, sgu_ref, stage_ref):
    gm = x_ref.shape[0]
    x = x_ref[...]
    hb = _rms(x, pre_ref[...]).astype(_BF16)

    z = _dot(hb, wz_ref[...])
    gate_a = jax.nn.sigmoid(_dot(hb, wga_ref[...]))

    z = 0.5 * z * (1.0 + lax.erf(z * (2.0 ** -0.5)))
    u, vv = z[:, :_SGU_W], z[:, _SGU_W:]
    mu = jnp.mean(vv, axis=-1, keepdims=True)
    var = jnp.mean(jnp.square(vv - mu), axis=-1, keepdims=True)
    vn = ((vv - mu) * lax.rsqrt(var + _EPS) * lng_ref[...] + lnb_ref[...]).astype(_BF16)
    gate_b = jax.nn.sigmoid(_dot(hb, wgb_ref[...]))

    def token_order(ref, d, slab, lanes):
        if d == 1:
            return ref[0, :, lanes]
        for r in range(d):
            stage_ref[slab, pl.ds(r, gm // d, stride=d), :] = ref[r, :, lanes]
        return stage_ref[slab]

    att_halves = []
    for half in range(_GROUP_W // _LANES):
        lanes = slice(half * _LANES, (half + 1) * _LANES)
        slab = 4 * half
        outs = (token_order(o1_ref, 1, 0, lanes), token_order(o4_ref, 4, slab, lanes),
                token_order(o16_ref, 16, slab + 1, lanes))
        lses = (token_order(l1_ref, 1, 0, lanes), token_order(l4_ref, 4, slab + 2, lanes),
                token_order(l16_ref, 16, slab + 3, lanes))
        top = jnp.maximum(jnp.maximum(lses[0], lses[1]), lses[2])
        wts = [jnp.exp(l - top) for l in lses]
        att_halves.append((wts[0] * outs[0] + wts[1] * outs[1] + wts[2] * outs[2])
                          / (wts[0] + wts[1] + wts[2]))
    att = jnp.concatenate(att_halves, axis=1)
    a = _dot(att.astype(_BF16), watt_ref[...])

    for c in range(gm // _CHUNK):
        rows = slice(c * _CHUNK, (c + 1) * _CHUNK)
        for g in range(_SGU_GROUPS):
            cols = slice(g * _LANES, (g + 1) * _LANES)
            mixed = _dot(ws_ref[g], vn[rows, cols]) + bs_ref[g]
            sgu_ref[rows, cols] = (u[rows, cols] * mixed).astype(_BF16)
    b = _dot(sgu_ref[...], wsgu_ref[...])

    merged = gate_a * a + gate_b * b
    y = _dot(merged.astype(_BF16), wout_ref[...])
    y_ref[...] = x + _rms(y, post_ref[...])


def _mix(x, outs, lses, segment,
         pre_g, post_g, wz, wga, wgb, ln_g, ln_b, ws, bs, watt, wsgu, wout):
    tm = _TOKEN_TILE
    tiles = segment // tm
    n_seg = x.shape[0] // segment
    row = pl.BlockSpec((tm, _D), lambda v, j: (v * tiles + j, 0))
    grp = [pl.BlockSpec((None, d, tm // d, _GROUP_W), lambda v, j: (v, 0, j, 0)) for d in _DILATIONS]
    consts = (pre_g, post_g, wz, wga, wgb, ln_g, ln_b, ws, bs, watt, wsgu, wout)
    n_stage = 2 * sum(d > 1 for d in _DILATIONS) * (_GROUP_W // _LANES)
    return pl.pallas_call(
        _mix_body,
        grid=(n_seg, tiles),
        in_specs=[row] + grp + grp + [_const_spec(c.shape) for c in consts],
        out_specs=row,
        out_shape=jax.ShapeDtypeStruct(x.shape, _F32),
        scratch_shapes=[pltpu.VMEM((tm, _SGU_W), _BF16), pltpu.VMEM((n_stage, tm, _LANES), _F32)],
        compiler_params=_params(2),
        name="mix",
    )(x, *outs, *lses, *consts)


def _row(v):
    return v.reshape(1, -1).astype(_F32)


def _ffn_weights(pre_g, post_g, fin_g, w_gate, w_up, w_down):
    return (_row(pre_g), _row(post_g), _row(fin_g),
            w_gate.astype(_BF16), w_up.astype(_BF16), w_down.astype(_BF16))


def kernel(x_prompt, x_sample, rel_bias, ffn1_pre_g, ffn1_post_g, ffn1_w_gate, ffn1_w_up, ffn1_w_down, mix_pre_g, w_in, sgu_ln_g, sgu_ln_b, sgu_w_s, sgu_b_s, w_att, w_sgu, w_out, mix_post_g, ffn2_pre_g, ffn2_post_g, ffn2_w_gate, ffn2_w_up, ffn2_w_down, final_g):
    assert ffn1_w_gate.shape[0] == 1, "one layer"
    l = 0
    xs = (x_prompt, x_sample)
    assert x_prompt.size == x_sample.size, "the two request groups hold the same number of tokens"
    segment = math.gcd(x_prompt.shape[1], x_sample.shape[1])
    assert segment % _TOKEN_TILE == 0 and (_TOKEN_TILE // _ROW_GROUPS) % (16 * max(_DILATIONS)) == 0
    layout = tuple((x.shape[0] * x.shape[1] // segment, x.shape[1] // segment) for x in xs)

    w_in_b = w_in[l].astype(_BF16)
    z0 = 3 * _QKV_W
    ffn1 = _ffn_weights(ffn1_pre_g[l], ffn1_post_g[l], final_g[l],
                        ffn1_w_gate[l], ffn1_w_up[l], ffn1_w_down[l])
    ffn2 = _ffn_weights(ffn2_pre_g[l], ffn2_post_g[l], final_g[l],
                        ffn2_w_gate[l], ffn2_w_up[l], ffn2_w_down[l])
    mix_w = (
        _row(mix_pre_g[l]), _row(mix_post_g[l]),
        w_in_b[:, z0:z0 + 2 * _SGU_W], w_in_b[:, z0 + 2 * _SGU_W:z0 + 2 * _SGU_W + _D],
        w_in_b[:, z0 + 2 * _SGU_W + _D:],
        _row(sgu_ln_g[l]), _row(sgu_ln_b[l]),
        sgu_w_s[l].astype(_BF16),
        jnp.broadcast_to(sgu_b_s[l].astype(_F32)[:, :, None], (_SGU_GROUPS, _CHUNK, _LANES)),
        w_att[l].astype(_BF16), w_sgu[l].astype(_BF16), w_out[l].astype(_BF16),
    )

    (x1,) = _ffn([x.reshape(-1, _D) for x in xs], *ffn1, n_out=1, final=False)
    qkv = _qkv(x1, _row(mix_pre_g[l]), w_in_b[:, :z0], segment)
    ng = len(_DILATIONS)
    outs, lses = [], []
    for g in range(ng):
        o, lse = _att(qkv[g], qkv[ng + g], qkv[2 * ng + g], _band_bias(rel_bias, g), layout)
        outs.append(o)
        lses.append(lse)
    x2 = _mix(x1, outs, lses, segment, *mix_w)
    ys = _ffn([x2], *ffn2, n_out=len(xs), final=True)
    return tuple(y.reshape(x.shape) for y, x in zip(ys, xs))
```

```python
import functools
import math

import numpy as np
import jax
import jax.numpy as jnp
from jax import lax
from jax.experimental import pallas as pl
from jax.experimental.pallas import tpu as pltpu

_D = 1024
_DFF = 2816
_HEAD_DIM = 64
_HEADS_PER_GROUP = 4
_DILATIONS = (1, 4, 16)
_HALF_KEYS = 64
_GROUP_W = _HEADS_PER_GROUP * _HEAD_DIM
_QKV_W = len(_DILATIONS) * _GROUP_W
_SGU_W = 512
_SGU_GROUPS = 4
_CHUNK = 128
_N_BUCKETS = 32
_MAX_DISTANCE = 1024
_EPS = 1e-6
_NEG_INF = -1e30
_ATT_SCALE = _HEAD_DIM ** -0.5

_LANES = 128
_Q_BLOCK = 128
_K_WINDOW = _Q_BLOCK + 2 * _HALF_KEYS
_ATT_UNITS_PER_STEP = 16
_FF_CHUNK = 256
_N_FF_CHUNKS = _DFF // _FF_CHUNK
_TOKEN_TILE = 1024
_ROW_GROUPS = 2
_VMEM_LIMIT = 56 * 1024 * 1024

_BF16 = jnp.bfloat16
_F32 = jnp.float32


def _rms(x, g):
    return x * lax.rsqrt(jnp.mean(x * x, axis=-1, keepdims=True) + _EPS) * g


def _dot(a, b):
    return jnp.dot(a, b, preferred_element_type=_F32)


def _const_spec(shape):
    return pl.BlockSpec(shape, lambda *_: (0,) * len(shape), pipeline_mode=pl.Buffered(1))


def _params(n_axes):
    return pltpu.CompilerParams(
        dimension_semantics=("arbitrary",) * n_axes, vmem_limit_bytes=_VMEM_LIMIT)


def _ffn_body(*refs, n_in, final):
    x_refs = refs[:n_in]
    pre_ref, post_ref, fin_ref, wg_ref, wu_ref, wd_ref, o_ref, xb_ref, acc_ref = refs[n_in:]
    step = pl.program_id(0)
    steps = pl.num_programs(0)
    tm = xb_ref.shape[0]
    for s in range(_ROW_GROUPS):
        rows = slice(s * tm // _ROW_GROUPS, (s + 1) * tm // _ROW_GROUPS)
        x = x_refs[0][rows, :]
        for k in range(1, n_in):
            x = jnp.where(step * n_in >= k * steps, x_refs[k][rows, :], x)
        xb_ref[rows, :] = _rms(x, pre_ref[...]).astype(_BF16)
        for c in range(_N_FF_CHUNKS):
            xb = xb_ref[rows, :]
            ff = slice(c * _FF_CHUNK, (c + 1) * _FF_CHUNK)
            gate = _dot(xb, wg_ref[:, ff])
            up = _dot(xb, wu_ref[:, ff])
            act = (gate * jax.nn.sigmoid(gate) * up).astype(_BF16)
            part = _dot(act, wd_ref[ff, :])
            if c == 0:
                acc_ref[rows, :] = part
            else:
                acc_ref[rows, :] += part
        y = x + 0.5 * _rms(acc_ref[rows, :], post_ref[...])
        if final:
            y = _rms(y, fin_ref[...])
        o_ref[rows, :] = y


def _ffn(xs, pre_g, post_g, fin_g, wg, wu, wd, *, final, first_tile=0, tiles=None):
    n_in = len(xs)
    tm = _TOKEN_TILE
    tiles = xs[0].shape[0] // tm if tiles is None else tiles
    assert all(x.shape == xs[0].shape for x in xs)

    def in_map(k):
        return lambda i: (first_tile + jnp.clip(i - k * tiles, 0, tiles - 1), 0)

    return pl.pallas_call(
        functools.partial(_ffn_body, n_in=n_in, final=final),
        grid=(n_in * tiles,),
        in_specs=[pl.BlockSpec((tm, _D), in_map(k)) for k in range(n_in)]
        + [_const_spec((1, _D)), _const_spec((1, _D)), _const_spec((1, _D)),
           _const_spec(wg.shape), _const_spec(wu.shape), _const_spec(wd.shape)],
        out_specs=pl.BlockSpec((tm, _D), lambda i: (i, 0)),
        out_shape=jax.ShapeDtypeStruct((n_in * tiles * tm, _D), _F32),
        scratch_shapes=[pltpu.VMEM((tm, _D), _BF16), pltpu.VMEM((tm, _D), _F32)],
        compiler_params=_params(1),
        name="ffn_final" if final else "ffn",
    )(*xs, pre_g, post_g, fin_g, wg, wu, wd)


def _qkv_body(x_ref, g_ref, w_ref, *refs):
    out_refs, stage_ref = refs[:-1], refs[-1]
    gm = x_ref.shape[0] // _ROW_GROUPS
    for s in range(_ROW_GROUPS):
        r0 = s * gm
        hb = _rms(x_ref[r0:r0 + gm, :], g_ref[...]).astype(_BF16)
        slab = 0
        for part in range(3):
            for g, d in enumerate(_DILATIONS):
                col = part * _QKV_W + g * _GROUP_W
                res = _dot(hb, w_ref[:, col:col + _GROUP_W])
                if part == 0:
                    res = res * _ATT_SCALE
                o_ref = out_refs[part * len(_DILATIONS) + g]
                if d == 1:
                    o_ref[0, r0:r0 + gm, :] = res.astype(_BF16)
                    continue
                for half in range(_GROUP_W // _LANES):
                    lanes = slice(half * _LANES, (half + 1) * _LANES)
                    stage_ref[slab, r0:r0 + gm, :] = res[:, lanes]
                    for r in range(d):
                        o_ref[r, r0 // d:(r0 + gm) // d, lanes] = stage_ref[
                            slab, pl.ds(r0 + r, gm // d, stride=d), :].astype(_BF16)
                    slab += 1


def _qkv(x, pre_g, w, segment):
    tm = _TOKEN_TILE
    tiles = segment // tm
    n_seg = x.shape[0] // segment
    n_stage = 3 * sum(d > 1 for d in _DILATIONS) * (_GROUP_W // _LANES)
    out_specs, out_shape = [], []
    for _ in range(3):
        for d in _DILATIONS:
            out_specs.append(pl.BlockSpec((None, d, tm // d, _GROUP_W), lambda v, j: (v, 0, j, 0)))
            out_shape.append(jax.ShapeDtypeStruct((n_seg, d, segment // d, _GROUP_W), _BF16))
    return pl.pallas_call(
        _qkv_body,
        grid=(n_seg, tiles),
        in_specs=[pl.BlockSpec((tm, _D), lambda v, j: (v * tiles + j, 0)), _const_spec((1, _D)),
                  _const_spec(w.shape)],
        out_specs=out_specs,
        out_shape=out_shape,
        scratch_shapes=[pltpu.VMEM((n_stage, tm, _LANES), _F32)],
        compiler_params=_params(2),
        name="qkv",
    )(x, pre_g, w)


def _att_body(q_ref, k_ref, v_ref, kp_ref, kn_ref, vp_ref, vn_ref, bias_ref, o_ref, lse_ref,
              *, seq, blocks, classes, layout):
    seg = pl.program_id(0)
    step = pl.program_id(2)
    last = seq // _Q_BLOCK - 1
    pos, per, base = seg, layout[0][1], 0
    for n_seg, seg_per_seq in layout:
        inside = (seg >= base) & (seg < base + n_seg)
        pos = jnp.where(inside, (seg - base) % seg_per_seq, pos)
        per = jnp.where(inside, seg_per_seq, per)
        base += n_seg
    has_prev = pos != 0
    has_next = pos != per - 1
    tail = _K_WINDOW - _HALF_KEYS
    low_head = lax.broadcasted_iota(jnp.int32, (_Q_BLOCK, _LANES), 1) < _HEAD_DIM
    for blk in range(blocks):
        i = step * blocks + blk
        start = jnp.clip(i * _Q_BLOCK - _HALF_KEYS, 0, seq - _K_WINDOW)
        start = pl.multiple_of(start, _HALF_KEYS)
        halo_lo = (i == 0) & has_prev
        halo_hi = (i == last) & has_next
        variant = jnp.where((i == 0) & jnp.logical_not(has_prev), 0,
                            jnp.where((i == last) & jnp.logical_not(has_next), 2, 1))
        rows = slice(blk * _Q_BLOCK, (blk + 1) * _Q_BLOCK)
        for r in range(classes):
            for pair in range(_GROUP_W // _LANES):
                cols = slice(pair * _LANES, (pair + 1) * _LANES)
                q2 = q_ref[r, rows, cols]
                k2 = k_ref[r, pl.ds(start, _K_WINDOW), cols]
                v2 = v_ref[r, pl.ds(start, _K_WINDOW), cols]
                if blk == 0:
                    k2 = jnp.where(halo_lo, jnp.concatenate(
                        [kp_ref[r, :, cols], k_ref[r, 0:tail, cols]], axis=0), k2)
                    v2 = jnp.where(halo_lo, jnp.concatenate(
                        [vp_ref[r, :, cols], v_ref[r, 0:tail, cols]], axis=0), v2)
                if blk == blocks - 1:
                    k2 = jnp.where(halo_hi, jnp.concatenate(
                        [k_ref[r, seq - tail:seq, cols], kn_ref[r, :, cols]], axis=0), k2)
                    v2 = jnp.where(halo_hi, jnp.concatenate(
                        [v_ref[r, seq - tail:seq, cols], vn_ref[r, :, cols]], axis=0), v2)
                zero = jnp.zeros_like(q2)
                qq = jnp.concatenate(
                    [jnp.where(low_head, q2, zero), jnp.where(low_head, zero, q2)], axis=0)
                s = lax.dot_general(qq, k2, (((1,), (1,)), ((), ())), preferred_element_type=_F32)
                s = s + bias_ref[variant, pair]
                m = jnp.max(s, axis=-1, keepdims=True)
                e = jnp.exp(s - m)
                den = jnp.sum(e, axis=-1, keepdims=True)
                out = _dot(e.astype(_BF16), v2) / den
                lse = jnp.broadcast_to(m + jnp.log(den), out.shape)
                o_ref[r, rows, cols] = jnp.where(low_head, out[:_Q_BLOCK], out[_Q_BLOCK:])
                lse_ref[r, rows, cols] = jnp.where(low_head, lse[:_Q_BLOCK], lse[_Q_BLOCK:])


def _att(q, k, v, bias, layout):
    n_seg, d, seq, _ = q.shape
    nb = seq // _Q_BLOCK
    assert seq % _Q_BLOCK == 0 and seq >= _K_WINDOW and sum(n for n, _ in layout) == n_seg
    blocks = min(_ATT_UNITS_PER_STEP, nb)
    classes = min(_ATT_UNITS_PER_STEP // blocks, d)
    assert nb % blocks == 0 and d % classes == 0 and blocks >= 2
    halo_blocks = seq // _HALF_KEYS
    blk = pl.BlockSpec((None, classes, blocks * _Q_BLOCK, _GROUP_W), lambda s, r, i: (s, r, i, 0))
    full = pl.BlockSpec((None, classes, seq, _GROUP_W), lambda s, r, i: (s, r, 0, 0))
    prev = pl.BlockSpec((None, classes, _HALF_KEYS, _GROUP_W),
                        lambda s, r, i: (jnp.maximum(s - 1, 0), r, halo_blocks - 1, 0))
    nxt = pl.BlockSpec((None, classes, _HALF_KEYS, _GROUP_W),
                       lambda s, r, i: (jnp.minimum(s + 1, n_seg - 1), r, 0, 0))
    shape = jax.ShapeDtypeStruct((n_seg, d, seq, _GROUP_W), _F32)
    return pl.pallas_call(
        functools.partial(_att_body, seq=seq, blocks=blocks, classes=classes, layout=layout),
        grid=(n_seg, d // classes, nb // blocks),
        in_specs=[blk, full, full, prev, nxt, prev, nxt, _const_spec(bias.shape)],
        out_specs=[blk, blk],
        out_shape=[shape, shape],
        compiler_params=_params(3),
        name=f"att_d{d}",
    )(q, k, v, k, k, v, v, bias)


def _rel_bucket(rel):
    nb = _N_BUCKETS // 2
    max_exact = nb // 2
    ret = (rel > 0).astype(np.int32) * nb
    n = np.abs(rel)
    large = max_exact + (np.log(np.maximum(n, 1) / max_exact) / np.log(_MAX_DISTANCE / max_exact)
                         * (nb - max_exact)).astype(np.int32)
    large = np.minimum(large, nb - 1)
    return ret + np.where(n < max_exact, n, large)


def _band_bias(rel_bias, group):
    off = np.arange(-_HALF_KEYS, _HALF_KEYS + 1, dtype=np.int32) * _DILATIONS[group]
    heads = slice(group * _HEADS_PER_GROUP, (group + 1) * _HEADS_PER_GROUP)
    per_key = rel_bias[_rel_bucket(off)][:, heads].T.astype(_F32)
    n_keys = 2 * _HALF_KEYS + 1
    period = _K_WINDOW + _Q_BLOCK
    tables = []
    for shift in (_HALF_KEYS, 0, -_HALF_KEYS):
        lead = _Q_BLOCK - 1 - shift
        v = jnp.pad(per_key, ((0, 0), (lead, period - lead - n_keys)), constant_values=_NEG_INF)
        skew = jnp.tile(v, (1, _Q_BLOCK))[:, :_Q_BLOCK * (period - 1)]
        skew = skew.reshape(_HEADS_PER_GROUP, _Q_BLOCK, period - 1)
        tables.append(skew[:, :, _Q_BLOCK - 1:])
    return jnp.stack(tables).reshape(3, _GROUP_W // _LANES, 2 * _Q_BLOCK, _K_WINDOW)


def _mix_body(x_ref, o1_ref, o4_ref, o16_ref, l1_ref, l4_ref, l16_ref,
              pre_ref, post_ref, wz_ref, wga_ref, wgb_ref, lng_ref, lnb_ref, ws_ref, bs_ref,
              watt_ref, wsgu_ref, wout_ref, y_ref, sgu_ref, stage_ref):
    gm = x_ref.shape[0]
    x = x_ref[...]
    hb = _rms(x, pre_ref[...]).astype(_BF16)

    z = _dot(hb, wz_ref[...])
    gate_a = jax.nn.sigmoid(_dot(hb, wga_ref[...]))

    z = 0.5 * z * (1.0 + lax.erf(z * (2.0 ** -0.5)))
    u, vv = z[:, :_SGU_W], z[:, _SGU_W:]
    mu = jnp.mean(vv, axis=-1, keepdims=True)
    var = jnp.mean(jnp.square(vv - mu), axis=-1, keepdims=True)
    vn = ((vv - mu) * lax.rsqrt(var + _EPS) * lng_ref[...] + lnb_ref[...]).astype(_BF16)
    gate_b = jax.nn.sigmoid(_dot(hb, wgb_ref[...]))

    def token_order(ref, d, slab, lanes):
        if d == 1:
            return ref[0, :, lanes]
        for r in range(d):
            stage_ref[slab, pl.ds(r, gm // d, stride=d), :] = ref[r, :, lanes]
        return stage_ref[slab]

    att_halves = []
    for half in range(_GROUP_W // _LANES):
        lanes = slice(half * _LANES, (half + 1) * _LANES)
        slab = 4 * half
        outs = (token_order(o1_ref, 1, 0, lanes), token_order(o4_ref, 4, slab, lanes),
                token_order(o16_ref, 16, slab + 1, lanes))
        lses = (token_order(l1_ref, 1, 0, lanes), token_order(l4_ref, 4, slab + 2, lanes),
                token_order(l16_ref, 16, slab + 3, lanes))
        top = jnp.maximum(jnp.maximum(lses[0], lses[1]), lses[2])
        wts = [jnp.exp(l - top) for l in lses]
        att_halves.append((wts[0] * outs[0] + wts[1] * outs[1] + wts[2] * outs[2])
                          / (wts[0] + wts[1] + wts[2]))
    att = jnp.concatenate(att_halves, axis=1)
    a = _dot(att.astype(_BF16), watt_ref[...])

    for c in range(gm // _CHUNK):
        rows = slice(c * _CHUNK, (c + 1) * _CHUNK)
        for g in range(_SGU_GROUPS):
            cols = slice(g * _LANES, (g + 1) * _LANES)
            mixed = _dot(ws_ref[g], vn[rows, cols]) + bs_ref[g]
            sgu_ref[rows, cols] = (u[rows, cols] * mixed).astype(_BF16)
    b = _dot(sgu_ref[...], wsgu_ref[...])

    merged = gate_a * a + gate_b * b
    y = _dot(merged.astype(_BF16), wout_ref[...])
    y_ref[...] = x + _rms(y, post_ref[...])


def _mix(x, outs, lses, segment,
         pre_g, post_g, wz, wga, wgb, ln_g, ln_b, ws, bs, watt, wsgu, wout):
    tm = _TOKEN_TILE
    tiles = segment // tm
    n_seg = x.shape[0] // segment
    row = pl.BlockSpec((tm, _D), lambda v, j: (v * tiles + j, 0))
    grp = [pl.BlockSpec((None, d, tm // d, _GROUP_W), lambda v, j: (v, 0, j, 0)) for d in _DILATIONS]
    consts = (pre_g, post_g, wz, wga, wgb, ln_g, ln_b, ws, bs, watt, wsgu, wout)
    n_stage = 2 * sum(d > 1 for d in _DILATIONS) * (_GROUP_W // _LANES)
    return pl.pallas_call(
        _mix_body,
        grid=(n_seg, tiles),
        in_specs=[row] + grp + grp + [_const_spec(c.shape) for c in consts],
        out_specs=row,
        out_shape=jax.ShapeDtypeStruct(x.shape, _F32),
        scratch_shapes=[pltpu.VMEM((tm, _SGU_W), _BF16), pltpu.VMEM((n_stage, tm, _LANES), _F32)],
        compiler_params=_params(2),
        name="mix",
    )(x, *outs, *lses, *consts)


def _row(v):
    return v.reshape(1, -1).astype(_F32)


def _ffn_weights(pre_g, post_g, fin_g, w_gate, w_up, w_down):
    return (_row(pre_g), _row(post_g), _row(fin_g),
            w_gate.astype(_BF16), w_up.astype(_BF16), w_down.astype(_BF16))


def kernel(x_prompt, x_sample, rel_bias, ffn1_pre_g, ffn1_post_g, ffn1_w_gate, ffn1_w_up, ffn1_w_down, mix_pre_g, w_in, sgu_ln_g, sgu_ln_b, sgu_w_s, sgu_b_s, w_att, w_sgu, w_out, mix_post_g, ffn2_pre_g, ffn2_post_g, ffn2_w_gate, ffn2_w_up, ffn2_w_down, final_g):
    assert ffn1_w_gate.shape[0] == 1, "one layer"
    l = 0
    xs = (x_prompt, x_sample)
    assert x_prompt.size == x_sample.size, "the two request groups hold the same number of tokens"
    segment = math.gcd(x_prompt.shape[1], x_sample.shape[1])
    assert segment % _TOKEN_TILE == 0 and (_TOKEN_TILE // _ROW_GROUPS) % (16 * max(_DILATIONS)) == 0
    layout = tuple((x.shape[0] * x.shape[1] // segment, x.shape[1] // segment) for x in xs)

    w_in_b = w_in[l].astype(_BF16)
    z0 = 3 * _QKV_W
    ffn1 = _ffn_weights(ffn1_pre_g[l], ffn1_post_g[l], final_g[l],
                        ffn1_w_gate[l], ffn1_w_up[l], ffn1_w_down[l])
    ffn2 = _ffn_weights(ffn2_pre_g[l], ffn2_post_g[l], final_g[l],
                        ffn2_w_gate[l], ffn2_w_up[l], ffn2_w_down[l])
    mix_w = (
        _row(mix_pre_g[l]), _row(mix_post_g[l]),
        w_in_b[:, z0:z0 + 2 * _SGU_W], w_in_b[:, z0 + 2 * _SGU_W:z0 + 2 * _SGU_W + _D],
        w_in_b[:, z0 + 2 * _SGU_W + _D:],
        _row(sgu_ln_g[l]), _row(sgu_ln_b[l]),
        sgu_w_s[l].astype(_BF16),
        jnp.broadcast_to(sgu_b_s[l].astype(_F32)[:, :, None], (_SGU_GROUPS, _CHUNK, _LANES)),
        w_att[l].astype(_BF16), w_sgu[l].astype(_BF16), w_out[l].astype(_BF16),
    )

    x1 = _ffn([x.reshape(-1, _D) for x in xs], *ffn1, final=False)
    qkv = _qkv(x1, _row(mix_pre_g[l]), w_in_b[:, :z0], segment)
    ng = len(_DILATIONS)
    outs, lses = [], []
    for g in range(ng):
        o, lse = _att(qkv[g], qkv[ng + g], qkv[2 * ng + g], _band_bias(rel_bias, g), layout)
        outs.append(o)
        lses.append(lse)
    x2 = _mix(x1, outs, lses, segment, *mix_w)
    tiles = x_prompt.size // _D // _TOKEN_TILE
    ys = [_ffn([x2], *ffn2, final=True, first_tile=k * tiles, tiles=tiles) for k in range(len(xs))]
    return tuple(y.reshape(x.shape) for y, x in zip(ys, xs))
```

```python
import functools
import math

import numpy as np
import jax
import jax.numpy as jnp
from jax import lax
from jax.experimental import pallas as pl
from jax.experimental.pallas import tpu as pltpu

_D = 1024
_DFF = 2816
_HEAD_DIM = 64
_HEADS_PER_GROUP = 4
_DILATIONS = (1, 4, 16)
_HALF_KEYS = 64
_GROUP_W = _HEADS_PER_GROUP * _HEAD_DIM
_QKV_W = len(_DILATIONS) * _GROUP_W
_SGU_W = 512
_SGU_GROUPS = 4
_CHUNK = 128
_N_BUCKETS = 32
_MAX_DISTANCE = 1024
_EPS = 1e-6
_NEG_INF = -1e30
_ATT_SCALE = _HEAD_DIM ** -0.5
_LOG2E = math.log2(math.e)

_LANES = 128
_Q_BLOCK = 128
_K_WINDOW = _Q_BLOCK + 2 * _HALF_KEYS
_ATT_UNITS_PER_STEP = 16
_FF_CHUNK = 256
_N_FF_CHUNKS = _DFF // _FF_CHUNK
_TOKEN_TILE = 1024
_ROW_GROUPS = 2
_VMEM_LIMIT = 56 * 1024 * 1024

_BF16 = jnp.bfloat16
_F32 = jnp.float32


def _rms(x, g):
    return x * lax.rsqrt(jnp.mean(x * x, axis=-1, keepdims=True) + _EPS) * g


def _sigmoid(x):
    return 0.5 + 0.5 * jnp.tanh(0.5 * x)


def _dot(a, b):
    return jnp.dot(a, b, preferred_element_type=_F32)


def _const_spec(shape):
    return pl.BlockSpec(shape, lambda *_: (0,) * len(shape), pipeline_mode=pl.Buffered(1))


def _params(n_axes):
    return pltpu.CompilerParams(
        dimension_semantics=("arbitrary",) * n_axes, vmem_limit_bytes=_VMEM_LIMIT)


def _ffn_body(*refs, n_in, final):
    x_refs = refs[:n_in]
    pre_ref, post_ref, fin_ref, wg_ref, wu_ref, wd_ref, o_ref, xb_ref, acc_ref = refs[n_in:]
    step = pl.program_id(0)
    steps = pl.num_programs(0)
    tm = xb_ref.shape[0]
    for s in range(_ROW_GROUPS):
        rows = slice(s * tm // _ROW_GROUPS, (s + 1) * tm // _ROW_GROUPS)
        x = x_refs[0][rows, :]
        for k in range(1, n_in):
            x = jnp.where(step * n_in >= k * steps, x_refs[k][rows, :], x)
        xb_ref[rows, :] = _rms(x, pre_ref[...]).astype(_BF16)
        for c in range(_N_FF_CHUNKS):
            xb = xb_ref[rows, :]
            ff = slice(c * _FF_CHUNK, (c + 1) * _FF_CHUNK)
            gate = _dot(xb, wg_ref[:, ff])
            up = _dot(xb, wu_ref[:, ff])
            act = (gate * _sigmoid(gate) * up).astype(_BF16)
            part = _dot(act, wd_ref[ff, :])
            if c == 0:
                acc_ref[rows, :] = part
            else:
                acc_ref[rows, :] += part
        y = x + 0.5 * _rms(acc_ref[rows, :], post_ref[...])
        if final:
            y = _rms(y, fin_ref[...])
        o_ref[rows, :] = y


def _ffn(xs, pre_g, post_g, fin_g, wg, wu, wd, *, final, first_tile=0, tiles=None):
    n_in = len(xs)
    tm = _TOKEN_TILE
    tiles = xs[0].shape[0] // tm if tiles is None else tiles
    assert all(x.shape == xs[0].shape for x in xs)

    def in_map(k):
        return lambda i: (first_tile + jnp.clip(i - k * tiles, 0, tiles - 1), 0)

    return pl.pallas_call(
        functools.partial(_ffn_body, n_in=n_in, final=final),
        grid=(n_in * tiles,),
        in_specs=[pl.BlockSpec((tm, _D), in_map(k)) for k in range(n_in)]
        + [_const_spec((1, _D)), _const_spec((1, _D)), _const_spec((1, _D)),
           _const_spec(wg.shape), _const_spec(wu.shape), _const_spec(wd.shape)],
        out_specs=pl.BlockSpec((tm, _D), lambda i: (i, 0)),
        out_shape=jax.ShapeDtypeStruct((n_in * tiles * tm, _D), _F32),
        scratch_shapes=[pltpu.VMEM((tm, _D), _BF16), pltpu.VMEM((tm, _D), _F32)],
        compiler_params=_params(1),
        name="ffn_final" if final else "ffn",
    )(*xs, pre_g, post_g, fin_g, wg, wu, wd)


def _qkv_body(x_ref, g_ref, w_ref, *refs):
    out_refs, stage_ref = refs[:-1], refs[-1]
    gm = x_ref.shape[0] // _ROW_GROUPS
    for s in range(_ROW_GROUPS):
        r0 = s * gm
        hb = _rms(x_ref[r0:r0 + gm, :], g_ref[...]).astype(_BF16)
        slab = 0
        for part in range(3):
            for g, d in enumerate(_DILATIONS):
                col = part * _QKV_W + g * _GROUP_W
                res = _dot(hb, w_ref[:, col:col + _GROUP_W])
                if part == 0:
                    res = res * (_ATT_SCALE * _LOG2E)
                o_ref = out_refs[part * len(_DILATIONS) + g]
                if d == 1:
                    o_ref[0, r0:r0 + gm, :] = res.astype(_BF16)
                    continue
                for half in range(_GROUP_W // _LANES):
                    lanes = slice(half * _LANES, (half + 1) * _LANES)
                    stage_ref[slab, r0:r0 + gm, :] = res[:, lanes]
                    for r in range(d):
                        o_ref[r, r0 // d:(r0 + gm) // d, lanes] = stage_ref[
                            slab, pl.ds(r0 + r, gm // d, stride=d), :].astype(_BF16)
                    slab += 1


def _qkv(x, pre_g, w, segment):
    tm = _TOKEN_TILE
    tiles = segment // tm
    n_seg = x.shape[0] // segment
    n_stage = 3 * sum(d > 1 for d in _DILATIONS) * (_GROUP_W // _LANES)
    out_specs, out_shape = [], []
    for _ in range(3):
        for d in _DILATIONS:
            out_specs.append(pl.BlockSpec((None, d, tm // d, _GROUP_W), lambda v, j: (v, 0, j, 0)))
            out_shape.append(jax.ShapeDtypeStruct((n_seg, d, segment // d, _GROUP_W), _BF16))
    return pl.pallas_call(
        _qkv_body,
        grid=(n_seg, tiles),
        in_specs=[pl.BlockSpec((tm, _D), lambda v, j: (v * tiles + j, 0)), _const_spec((1, _D)),
                  _const_spec(w.shape)],
        out_specs=out_specs,
        out_shape=out_shape,
        scratch_shapes=[pltpu.VMEM((n_stage, tm, _LANES), _F32)],
        compiler_params=_params(2),
        name="qkv",
    )(x, pre_g, w)


def _att_body(q_ref, k_ref, v_ref, kp_ref, kn_ref, vp_ref, vn_ref, bias_ref, o_ref, lse_ref,
              *, seq, blocks, classes, layout):
    seg = pl.program_id(0)
    step = pl.program_id(2)
    last = seq // _Q_BLOCK - 1
    pos, per, base = seg, layout[0][1], 0
    for n_seg, seg_per_seq in layout:
        inside = (seg >= base) & (seg < base + n_seg)
        pos = jnp.where(inside, (seg - base) % seg_per_seq, pos)
        per = jnp.where(inside, seg_per_seq, per)
        base += n_seg
    has_prev = pos != 0
    has_next = pos != per - 1
    tail = _K_WINDOW - _HALF_KEYS
    low_head = lax.broadcasted_iota(jnp.int32, (_Q_BLOCK, _LANES), 1) < _HEAD_DIM
    for blk in range(blocks):
        i = step * blocks + blk
        start = jnp.clip(i * _Q_BLOCK - _HALF_KEYS, 0, seq - _K_WINDOW)
        start = pl.multiple_of(start, _HALF_KEYS)
        halo_lo = (i == 0) & has_prev
        halo_hi = (i == last) & has_next
        variant = jnp.where((i == 0) & jnp.logical_not(has_prev), 0,
                            jnp.where((i == last) & jnp.logical_not(has_next), 2, 1))
        rows = slice(blk * _Q_BLOCK, (blk + 1) * _Q_BLOCK)
        for r in range(classes):
            for pair in range(_GROUP_W // _LANES):
                cols = slice(pair * _LANES, (pair + 1) * _LANES)
                q2 = q_ref[r, rows, cols]
                k2 = k_ref[r, pl.ds(start, _K_WINDOW), cols]
                v2 = v_ref[r, pl.ds(start, _K_WINDOW), cols]
                if blk == 0:
                    k2 = jnp.where(halo_lo, jnp.concatenate(
                        [kp_ref[r, :, cols], k_ref[r, 0:tail, cols]], axis=0), k2)
                    v2 = jnp.where(halo_lo, jnp.concatenate(
                        [vp_ref[r, :, cols], v_ref[r, 0:tail, cols]], axis=0), v2)
                if blk == blocks - 1:
                    k2 = jnp.where(halo_hi, jnp.concatenate(
                        [k_ref[r, seq - tail:seq, cols], kn_ref[r, :, cols]], axis=0), k2)
                    v2 = jnp.where(halo_hi, jnp.concatenate(
                        [v_ref[r, seq - tail:seq, cols], vn_ref[r, :, cols]], axis=0), v2)
                zero = jnp.zeros_like(q2)
                qq = jnp.concatenate(
                    [jnp.where(low_head, q2, zero), jnp.where(low_head, zero, q2)], axis=0)
                s = lax.dot_general(qq, k2, (((1,), (1,)), ((), ())), preferred_element_type=_F32)
                s = s + bias_ref[variant, pair]
                m = jnp.max(s, axis=-1, keepdims=True)
                e = jnp.exp2(s - m)
                den = jnp.sum(e, axis=-1, keepdims=True)
                out = _dot(e.astype(_BF16), v2) / den
                lse = jnp.broadcast_to(m + jnp.log(den) * _LOG2E, out.shape)
                o_ref[r, rows, cols] = jnp.where(low_head, out[:_Q_BLOCK], out[_Q_BLOCK:])
                lse_ref[r, rows, cols] = jnp.where(low_head, lse[:_Q_BLOCK], lse[_Q_BLOCK:])


def _att(q, k, v, bias, layout):
    n_seg, d, seq, _ = q.shape
    nb = seq // _Q_BLOCK
    assert seq % _Q_BLOCK == 0 and seq >= _K_WINDOW and sum(n for n, _ in layout) == n_seg
    blocks = min(_ATT_UNITS_PER_STEP, nb)
    classes = min(_ATT_UNITS_PER_STEP // blocks, d)
    assert nb % blocks == 0 and d % classes == 0 and blocks >= 2
    halo_blocks = seq // _HALF_KEYS
    blk = pl.BlockSpec((None, classes, blocks * _Q_BLOCK, _GROUP_W), lambda s, r, i: (s, r, i, 0))
    full = pl.BlockSpec((None, classes, seq, _GROUP_W), lambda s, r, i: (s, r, 0, 0))
    prev = pl.BlockSpec((None, classes, _HALF_KEYS, _GROUP_W),
                        lambda s, r, i: (jnp.maximum(s - 1, 0), r, halo_blocks - 1, 0))
    nxt = pl.BlockSpec((None, classes, _HALF_KEYS, _GROUP_W),
                       lambda s, r, i: (jnp.minimum(s + 1, n_seg - 1), r, 0, 0))
    shape = jax.ShapeDtypeStruct((n_seg, d, seq, _GROUP_W), _F32)
    return pl.pallas_call(
        functools.partial(_att_body, seq=seq, blocks=blocks, classes=classes, layout=layout),
        grid=(n_seg, d // classes, nb // blocks),
        in_specs=[blk, full, full, prev, nxt, prev, nxt, _const_spec(bias.shape)],
        out_specs=[blk, blk],
        out_shape=[shape, shape],
        compiler_params=_params(3),
        name=f"att_d{d}",
    )(q, k, v, k, k, v, v, bias)


def _rel_bucket(rel):
    nb = _N_BUCKETS // 2
    max_exact = nb // 2
    ret = (rel > 0).astype(np.int32) * nb
    n = np.abs(rel)
    large = max_exact + (np.log(np.maximum(n, 1) / max_exact) / np.log(_MAX_DISTANCE / max_exact)
                         * (nb - max_exact)).astype(np.int32)
    large = np.minimum(large, nb - 1)
    return ret + np.where(n < max_exact, n, large)


def _band_bias(rel_bias, group):
    off = np.arange(-_HALF_KEYS, _HALF_KEYS + 1, dtype=np.int32) * _DILATIONS[group]
    heads = slice(group * _HEADS_PER_GROUP, (group + 1) * _HEADS_PER_GROUP)
    per_key = rel_bias[_rel_bucket(off)][:, heads].T.astype(_F32) * _LOG2E
    n_keys = 2 * _HALF_KEYS + 1
    period = _K_WINDOW + _Q_BLOCK
    tables = []
    for shift in (_HALF_KEYS, 0, -_HALF_KEYS):
        lead = _Q_BLOCK - 1 - shift
        v = jnp.pad(per_key, ((0, 0), (lead, period - lead - n_keys)), constant_values=_NEG_INF)
        skew = jnp.tile(v, (1, _Q_BLOCK))[:, :_Q_BLOCK * (period - 1)]
        skew = skew.reshape(_HEADS_PER_GROUP, _Q_BLOCK, period - 1)
        tables.append(skew[:, :, _Q_BLOCK - 1:])
    return jnp.stack(tables).reshape(3, _GROUP_W // _LANES, 2 * _Q_BLOCK, _K_WINDOW)


def _mix_body(x_ref, o1_ref, o4_ref, o16_ref, l1_ref, l4_ref, l16_ref,
              pre_ref, post_ref, wz_ref, wga_ref, wgb_ref, lng_ref, lnb_ref, ws_ref, bs_ref,
              watt_ref, wsgu_ref, wout_ref, y_ref, sgu_ref, stage_ref):
    gm = x_ref.shape[0]
    x = x_ref[...]
    hb = _rms(x, pre_ref[...]).astype(_BF16)

    z = _dot(hb, wz_ref[...])
    gate_a = _sigmoid(_dot(hb, wga_ref[...]))

    z = 0.5 * z * (1.0 + lax.erf(z * (2.0 ** -0.5)))
    u, vv = z[:, :_SGU_W], z[:, _SGU_W:]
    mu = jnp.mean(vv, axis=-1, keepdims=True)
    var = jnp.mean(jnp.square(vv - mu), axis=-1, keepdims=True)
    vn = ((vv - mu) * lax.rsqrt(var + _EPS) * lng_ref[...] + lnb_ref[...]).astype(_BF16)
    gate_b = _sigmoid(_dot(hb, wgb_ref[...]))

    def token_order(ref, d, slab, lanes):
        if d == 1:
            return ref[0, :, lanes]
        for r in range(d):
            stage_ref[slab, pl.ds(r, gm // d, stride=d), :] = ref[r, :, lanes]
        return stage_ref[slab]

    att_halves = []
    for half in range(_GROUP_W // _LANES):
        lanes = slice(half * _LANES, (half + 1) * _LANES)
        slab = 4 * half
        outs = (token_order(o1_ref, 1, 0, lanes), token_order(o4_ref, 4, slab, lanes),
                token_order(o16_ref, 16, slab + 1, lanes))
        lses = (token_order(l1_ref, 1, 0, lanes), token_order(l4_ref, 4, slab + 2, lanes),
                token_order(l16_ref, 16, slab + 3, lanes))
        top = jnp.maximum(jnp.maximum(lses[0], lses[1]), lses[2])
        wts = [jnp.exp2(l - top) for l in lses]
        att_halves.append((wts[0] * outs[0] + wts[1] * outs[1] + wts[2] * outs[2])
                          / (wts[0] + wts[1] + wts[2]))
    att = jnp.concatenate(att_halves, axis=1)
    a = _dot(att.astype(_BF16), watt_ref[...])

    for c in range(gm // _CHUNK):
        rows = slice(c * _CHUNK, (c + 1) * _CHUNK)
        for g in range(_SGU_GROUPS):
            cols = slice(g * _LANES, (g + 1) * _LANES)
            mixed = _dot(ws_ref[g], vn[rows, cols]) + bs_ref[g]
            sgu_ref[rows, cols] = (u[rows, cols] * mixed).astype(_BF16)
    b = _dot(sgu_ref[...], wsgu_ref[...])

    merged = gate_a * a + gate_b * b
    y = _dot(merged.astype(_BF16), wout_ref[...])
    y_ref[...] = x + _rms(y, post_ref[...])


def _mix(x, outs, lses, segment,
         pre_g, post_g, wz, wga, wgb, ln_g, ln_b, ws, bs, watt, wsgu, wout):
    tm = _TOKEN_TILE
    tiles = segment // tm
    n_seg = x.shape[0] // segment
    row = pl.BlockSpec((tm, _D), lambda v, j: (v * tiles + j, 0))
    grp = [pl.BlockSpec((None, d, tm // d, _GROUP_W), lambda v, j: (v, 0, j, 0)) for d in _DILATIONS]
    consts = (pre_g, post_g, wz, wga, wgb, ln_g, ln_b, ws, bs, watt, wsgu, wout)
    n_stage = 2 * sum(d > 1 for d in _DILATIONS) * (_GROUP_W // _LANES)
    return pl.pallas_call(
        _mix_body,
        grid=(n_seg, tiles),
        in_specs=[row] + grp + grp + [_const_spec(c.shape) for c in consts],
        out_specs=row,
        out_shape=jax.ShapeDtypeStruct(x.shape, _F32),
        scratch_shapes=[pltpu.VMEM((tm, _SGU_W), _BF16), pltpu.VMEM((n_stage, tm, _LANES), _F32)],
        compiler_params=_params(2),
        name="mix",
    )(x, *outs, *lses, *consts)


def _row(v):
    return v.reshape(1, -1).astype(_F32)


def _ffn_weights(pre_g, post_g, fin_g, w_gate, w_up, w_down):
    return (_row(pre_g), _row(post_g), _row(fin_g),
            w_gate.astype(_BF16), w_up.astype(_BF16), w_down.astype(_BF16))


def kernel(x_prompt, x_sample, rel_bias, ffn1_pre_g, ffn1_post_g, ffn1_w_gate, ffn1_w_up, ffn1_w_down, mix_pre_g, w_in, sgu_ln_g, sgu_ln_b, sgu_w_s, sgu_b_s, w_att, w_sgu, w_out, mix_post_g, ffn2_pre_g, ffn2_post_g, ffn2_w_gate, ffn2_w_up, ffn2_w_down, final_g):
    assert ffn1_w_gate.shape[0] == 1, "one layer"
    l = 0
    xs = (x_prompt, x_sample)
    assert x_prompt.size == x_sample.size, "the two request groups hold the same number of tokens"
    segment = math.gcd(x_prompt.shape[1], x_sample.shape[1])
    assert segment % _TOKEN_TILE == 0 and (_TOKEN_TILE // _ROW_GROUPS) % (16 * max(_DILATIONS)) == 0
    layout = tuple((x.shape[0] * x.shape[1] // segment, x.shape[1] // segment) for x in xs)

    w_in_b = w_in[l].astype(_BF16)
    z0 = 3 * _QKV_W
    ffn1 = _ffn_weights(ffn1_pre_g[l], ffn1_post_g[l], final_g[l],
                        ffn1_w_gate[l], ffn1_w_up[l], ffn1_w_down[l])
    ffn2 = _ffn_weights(ffn2_pre_g[l], ffn2_post_g[l], final_g[l],
                        ffn2_w_gate[l], ffn2_w_up[l], ffn2_w_down[l])
    mix_w = (
        _row(mix_pre_g[l]), _row(mix_post_g[l]),
        w_in_b[:, z0:z0 + 2 * _SGU_W], w_in_b[:, z0 + 2 * _SGU_W:z0 + 2 * _SGU_W + _D],
        w_in_b[:, z0 + 2 * _SGU_W + _D:],
        _row(sgu_ln_g[l]), _row(sgu_ln_b[l]),
        sgu_w_s[l].astype(_BF16),
        jnp.broadcast_to(sgu_b_s[l].astype(_F32)[:, :, None], (_SGU_GROUPS, _CHUNK, _LANES)),
        w_att[l].astype(_BF16), w_sgu[l].astype(_BF16), w_out[l].astype(_BF16),
    )

    x1 = _ffn([x.reshape(-1, _D) for x in xs], *ffn1, final=False)
    qkv = _qkv(x1, _row(mix_pre_g[l]), w_in_b[:, :z0], segment)
    ng = len(_DILATIONS)
    outs, lses = [], []
    for g in range(ng):
        o, lse = _att(qkv[g], qkv[ng + g], qkv[2 * ng + g], _band_bias(rel_bias, g), layout)
        outs.append(o)
        lses.append(lse)
    x2 = _mix(x1, outs, lses, segment, *mix_w)
    tiles = x_prompt.size // _D // _TOKEN_TILE
    ys = [_ffn([x2], *ffn2, final=True, first_tile=k * tiles, tiles=tiles) for k in range(len(xs))]
    return tuple(y.reshape(x.shape) for y, x in zip(ys, xs))
```
